```python
import math
import jax, jax.numpy as jnp
from jax import lax
import numpy as np

D_MODEL = 2048
BATCH = 16
SEQ = 2048
DEPTH = 4
DEC_BATCH = 32
DEC_SEQ = 64
PAST_LEN = 2048

CHUNK = 64
QBLK = 128
EPS = 1e-6
D_SSM = D_MODEL // 2
SSM_GROUP = 16
G_SSM = D_SSM // SSM_GROUP
P_SSM = 64
DT_MIN = 0.001
DT_MAX = 0.1
H_DIFF = D_MODEL // 512
DQK_DIFF = 64
DV_DIFF = 2 * DQK_DIFF
H_FOX = D_MODEL // 256
HD_FOX = 64
D_MIX = D_SSM + H_DIFF * DV_DIFF + H_FOX * HD_FOX
OFF_U = 0
OFF_QD = OFF_U + D_SSM
OFF_KD = OFF_QD + H_DIFF * 2 * DQK_DIFF
OFF_VD = OFF_KD + H_DIFF * 2 * DQK_DIFF
OFF_QF = OFF_VD + H_DIFF * DV_DIFF
OFF_KF = OFF_QF + H_FOX * HD_FOX
OFF_VF = OFF_KF + H_FOX * HD_FOX
OFF_FG = OFF_VF + H_FOX * HD_FOX
D_IN = OFF_FG + H_FOX
N_BUCKETS = 32
MAX_DISTANCE = 128
D_FF = ((8 * D_MODEL // 3 + 127) // 128) * 128
CONV_W = 3

kernel_name = 'hybrid_stream_encoder_step'


def rms(x32):
    return x32 * lax.rsqrt(jnp.mean(x32 * x32, axis=-1, keepdims=True) + EPS)


def rmsnorm(x, g):
    return (rms(x.astype(jnp.float32)) * g.astype(jnp.float32)).astype(x.dtype)


def t5_bucket(rel):
    nb = N_BUCKETS // 2
    max_exact = nb // 2
    n = jnp.abs(rel)
    large = max_exact + (jnp.log(jnp.maximum(n, 1).astype(jnp.float32) / max_exact)
                         / math.log(MAX_DISTANCE / max_exact) * (nb - max_exact)).astype(jnp.int32)
    large = jnp.minimum(large, nb - 1)
    return jnp.where(rel > 0, nb, 0) + jnp.where(n < max_exact, n, large)


def sweep_queries(block_fn, qs, q_pos):
    T = q_pos.shape[0]
    if T <= QBLK:
        return block_fn(qs, q_pos)
    n = T // QBLK
    qb = tuple(jnp.moveaxis(a.reshape(a.shape[0], n, QBLK, *a.shape[2:]), 1, 0) for a in qs)
    pb = q_pos.reshape(n, QBLK)
    out = lax.map(lambda args: block_fn(args[0], args[1]), (qb, pb))
    out = jnp.moveaxis(out, 0, 1)
    return out.reshape(out.shape[0], T, *out.shape[3:])


def s5_scan(u, h_re, h_im, lam_re, lam_im, log_dt, b_re, b_im, c_re, c_im, d_skip):
    f32 = jnp.float32
    u = u.astype(f32)
    lam_re, lam_im, b_re, b_im, c_re, c_im, d_skip = [a.astype(f32) for a in
                                                      (lam_re, lam_im, b_re, b_im, c_re, c_im, d_skip)]
    bsz, L = u.shape[0], u.shape[1]
    dt = jnp.exp(log_dt.astype(f32))[:, None]
    mag = jnp.exp(lam_re * dt)
    ar, ai = mag * jnp.cos(lam_im * dt), mag * jnp.sin(lam_im * dt)
    den = lam_re * lam_re + lam_im * lam_im
    fr = ((ar - 1.0) * lam_re + ai * lam_im) / den
    fi = (ai * lam_re - (ar - 1.0) * lam_im) / den
    bb_re = fr[..., None] * b_re - fi[..., None] * b_im
    bb_im = fr[..., None] * b_im + fi[..., None] * b_re
    blk = min(CHUNK, L)
    n = L // blk
    k = jnp.arange(1, blk + 1, dtype=f32)[:, None, None]
    pmag = jnp.exp(lam_re * dt * k)
    pw_re = pmag * jnp.cos(lam_im * dt * k)
    pw_im = pmag * jnp.sin(lam_im * dt * k)
    a_re = jnp.broadcast_to(ar, (bsz, blk) + ar.shape)
    a_im = jnp.broadcast_to(ai, (bsz, blk) + ai.shape)
    ub = jnp.moveaxis(u.reshape(bsz, n, blk, G_SSM, SSM_GROUP), 1, 0)

    def combine(e1, e2):
        a1r, a1i, b1r, b1i = e1
        a2r, a2i, b2r, b2i = e2
        return (a2r * a1r - a2i * a1i, a2r * a1i + a2i * a1r,
                a2r * b1r - a2i * b1i + b2r, a2r * b1i + a2i * b1r + b2i)

    def step(carry, uc):
        hr, hi = carry
        bur = jnp.einsum('btgh,gph->btgp', uc, bb_re)
        bui = jnp.einsum('btgh,gph->btgp', uc, bb_im)
        _, _, xr, xi = lax.associative_scan(combine, (a_re, a_im, bur, bui), axis=1)
        xr = xr + pw_re * hr[:, None] - pw_im * hi[:, None]
        xi = xi + pw_re * hi[:, None] + pw_im * hr[:, None]
        y = (jnp.einsum('btgp,ghp->btgh', xr, c_re) - jnp.einsum('btgp,ghp->btgh', xi, c_im)
             + d_skip * uc)
        return (xr[:, -1], xi[:, -1]), y

    (hr, hi), ys = lax.scan(step, (h_re.astype(f32), h_im.astype(f32)), ub)
    y = jnp.moveaxis(ys, 0, 1).reshape(bsz, L, G_SSM * SSM_GROUP)
    return y, hr, hi


def layer(x, st, p, lam_init):
    f32 = jnp.float32
    kd_c, vd_c, kf_c, vf_c, lf_c, sr_c, si_c, cv_c = st
    bsz, T = x.shape[0], x.shape[1]
    past = kd_c.shape[1]
    q_pos = past + jnp.arange(T, dtype=jnp.int32)
    k_pos = jnp.arange(past + T, dtype=jnp.int32)
    z = rmsnorm(x, p['g_norm_mix']) @ p['w_in']

    u = z[..., OFF_U:OFF_QD].reshape(bsz, T, G_SSM, SSM_GROUP)
    y_ssm, sr, si = s5_scan(u, sr_c, si_c, p['ssm_lam_re'], p['ssm_lam_im'], p['ssm_log_dt'],
                            p['ssm_b_re'], p['ssm_b_im'], p['ssm_c_re'], p['ssm_c_im'], p['ssm_d'])
    g = jax.nn.gelu(y_ssm)
    y_ssm = g * jax.nn.sigmoid(g @ p['w_glu'].astype(f32) + p['b_glu'].astype(f32))
    o_ssm = rms(y_ssm)

    qd = z[..., OFF_QD:OFF_KD].reshape(bsz, T, H_DIFF, 2, DQK_DIFF)
    kd = z[..., OFF_KD:OFF_VD].reshape(bsz, T, H_DIFF, 2, DQK_DIFF)
    vd = z[..., OFF_VD:OFF_QF].reshape(bsz, T, H_DIFF, DV_DIFF)
    kd_all = jnp.concatenate([kd_c.astype(kd.dtype), kd], axis=1).astype(f32)
    vd_all = jnp.concatenate([vd_c.astype(vd.dtype), vd], axis=1).astype(f32)
    lam = (jnp.exp(jnp.sum(p['diff_lam_q1'].astype(f32) * p['diff_lam_k1'].astype(f32)))
           - jnp.exp(jnp.sum(p['diff_lam_q2'].astype(f32) * p['diff_lam_k2'].astype(f32))) + lam_init)
    rel_bias = p['rel_bias'].astype(f32)
    k_chunk = k_pos // CHUNK

    def diff_block(qs, qp):
        q = qs[0]
        s = jnp.einsum('bqhmd,bkhmd->bhmqk', q.astype(f32), kd_all) * DQK_DIFF ** -0.5
        bias = jnp.transpose(rel_bias[t5_bucket(k_pos[None, :] - qp[:, None])], (2, 0, 1))
        mask = k_chunk[None, :] <= (qp // CHUNK)[:, None]
        s = jnp.where(mask, s + bias[None, :, None], -jnp.inf)
        prob = jax.nn.softmax(s, axis=-1)
        a = prob[:, :, 0] - lam * prob[:, :, 1]
        return jnp.einsum('bhqk,bkhd->bqhd', a, vd_all)

    o_diff = rms(sweep_queries(diff_block, (qd,), q_pos)) * (1.0 - lam_init)

    qf = z[..., OFF_QF:OFF_KF].reshape(bsz, T, H_FOX, HD_FOX)
    kf = z[..., OFF_KF:OFF_VF].reshape(bsz, T, H_FOX, HD_FOX)
    vf = z[..., OFF_VF:OFF_FG].reshape(bsz, T, H_FOX, HD_FOX)
    logf = jax.nn.log_sigmoid(z[..., OFF_FG:D_IN].astype(f32) + p['fox_b_f'].astype(f32))
    kf_all = jnp.concatenate([kf_c.astype(kf.dtype), kf], axis=1).astype(f32)
    vf_all = jnp.concatenate([vf_c.astype(vf.dtype), vf], axis=1).astype(f32)
    cum = jnp.cumsum(jnp.concatenate([lf_c.astype(f32), logf], axis=1), axis=1)
    cum_k = jnp.transpose(cum, (0, 2, 1))

    def fox_block(qs, qp):
        q, cq = qs
        s = (jnp.einsum('bqhd,bkhd->bhqk', q.astype(f32), kf_all) * HD_FOX ** -0.5
             + jnp.transpose(cq, (0, 2, 1))[..., None] - cum_k[:, :, None, :])
        s = jnp.where(k_pos[None, :] <= qp[:, None], s, -jnp.inf)
        return jnp.einsum('bhqk,bkhd->bqhd', jax.nn.softmax(s, axis=-1), vf_all)

    o_fox = rms(sweep_queries(fox_block, (qf, cum[:, past:]), q_pos))

    mix = jnp.concatenate([o_ssm, o_diff.reshape(bsz, T, -1), o_fox.reshape(bsz, T, -1)], axis=-1)
    mix = mix * p['g_mix_out'].astype(f32)
    x = x + mix.astype(x.dtype) @ p['w_o']

    ab = rmsnorm(x, p['g_norm_ffn']) @ p['w_ffn_in']
    a, b = ab[..., :D_FF], ab[..., D_FF:]
    a_pad = jnp.concatenate([cv_c.astype(a.dtype), a], axis=1)
    ac = lax.conv_general_dilated(a_pad, p['ffn_conv_w'][:, None, :].astype(a.dtype), (1,), 'VALID',
                                  dimension_numbers=('NWC', 'WIO', 'NWC'), feature_group_count=D_FF)
    hh = jax.nn.silu(ac + p['ffn_conv_b'].astype(a.dtype)) * b
    x = x + hh @ p['w_ffn_out']
    new_st = (kd, vd, kf, vf, logf, sr, si, a_pad[:, -(CONV_W - 1):])
    return x, new_st


def setup_inputs(seed: int = 0) -> dict:
    key = jax.random.key(seed)
    ks = iter(jax.random.split(key, 48))
    f32 = jnp.float32

    def nrm(shape, scale):
        return scale * jax.random.normal(next(ks), shape, f32)

    def gain(shape):
        return 1.0 + 0.1 * jax.random.normal(next(ks), shape, f32)

    x_prompt = nrm((BATCH, SEQ, D_MODEL), 1.0)
    x_sample = nrm((DEC_BATCH, DEC_SEQ, D_MODEL), 1.0)
    cache_diff_k = nrm((DEPTH, DEC_BATCH, PAST_LEN, H_DIFF, 2, DQK_DIFF), 1.0)
    cache_diff_v = nrm((DEPTH, DEC_BATCH, PAST_LEN, H_DIFF, DV_DIFF), 1.0)
    cache_fox_k = nrm((DEPTH, DEC_BATCH, PAST_LEN, H_FOX, HD_FOX), 1.0)
    cache_fox_v = nrm((DEPTH, DEC_BATCH, PAST_LEN, H_FOX, HD_FOX), 1.0)
    cache_fox_logf = jax.nn.log_sigmoid(2.0 + nrm((DEPTH, DEC_BATCH, PAST_LEN, H_FOX), 1.0))
    state_ssm_re = nrm((DEPTH, DEC_BATCH, G_SSM, P_SSM), 0.5)
    state_ssm_im = nrm((DEPTH, DEC_BATCH, G_SSM, P_SSM), 0.5)
    state_ffn_conv = nrm((DEPTH, DEC_BATCH, CONV_W - 1, D_FF), 1.0)
    g_norm_mix = gain((DEPTH, D_MODEL))
    w_in = nrm((DEPTH, D_MODEL, D_IN), D_MODEL ** -0.5)
    ssm_lam_re = -0.5 + nrm((DEPTH, G_SSM, P_SSM), 0.01)
    ssm_lam_im = math.pi * jnp.arange(P_SSM, dtype=f32) + nrm((DEPTH, G_SSM, P_SSM), 0.01)
    ssm_log_dt = jax.random.uniform(next(ks), (DEPTH, G_SSM), f32, math.log(DT_MIN), math.log(DT_MAX))
    ssm_b_re = nrm((DEPTH, G_SSM, P_SSM, SSM_GROUP), (2 * SSM_GROUP) ** -0.5)
    ssm_b_im = nrm((DEPTH, G_SSM, P_SSM, SSM_GROUP), (2 * SSM_GROUP) ** -0.5)
    ssm_c_re = nrm((DEPTH, G_SSM, SSM_GROUP, P_SSM), (2 * P_SSM) ** -0.5)
    ssm_c_im = nrm((DEPTH, G_SSM, SSM_GROUP, P_SSM), (2 * P_SSM) ** -0.5)
    ssm_d = nrm((DEPTH, G_SSM, SSM_GROUP), 1.0)
    w_glu = nrm((DEPTH, D_SSM, D_SSM), D_SSM ** -0.5)
    b_glu = nrm((DEPTH, D_SSM), 0.01)
    diff_lam_q1 = nrm((DEPTH, DQK_DIFF), 0.1)
    diff_lam_k1 = nrm((DEPTH, DQK_DIFF), 0.1)
    diff_lam_q2 = nrm((DEPTH, DQK_DIFF), 0.1)
    diff_lam_k2 = nrm((DEPTH, DQK_DIFF), 0.1)
    rel_bias = nrm((N_BUCKETS, H_DIFF), 0.5)
    fox_b_f = 2.0 + nrm((DEPTH, H_FOX), 0.5)
    g_mix_out = gain((DEPTH, D_MIX))
    w_o = nrm((DEPTH, D_MIX, D_MODEL), D_MIX ** -0.5)
    g_norm_ffn = gain((DEPTH, D_MODEL))
    w_ffn_in = nrm((DEPTH, D_MODEL, 2 * D_FF), D_MODEL ** -0.5)
    ffn_conv_w = nrm((DEPTH, CONV_W, D_FF), CONV_W ** -0.5)
    ffn_conv_b = nrm((DEPTH, D_FF), 0.01)
    w_ffn_out = nrm((DEPTH, D_FF, D_MODEL), D_FF ** -0.5)
    g_final = gain((D_MODEL,))
    return {'x_prompt': x_prompt, 'x_sample': x_sample,
            'cache_diff_k': cache_diff_k, 'cache_diff_v': cache_diff_v,
            'cache_fox_k': cache_fox_k, 'cache_fox_v': cache_fox_v, 'cache_fox_logf': cache_fox_logf,
            'state_ssm_re': state_ssm_re, 'state_ssm_im': state_ssm_im, 'state_ffn_conv': state_ffn_conv,
            'g_norm_mix': g_norm_mix, 'w_in': w_in,
            'ssm_lam_re': ssm_lam_re, 'ssm_lam_im': ssm_lam_im, 'ssm_log_dt': ssm_log_dt,
            'ssm_b_re': ssm_b_re, 'ssm_b_im': ssm_b_im, 'ssm_c_re': ssm_c_re, 'ssm_c_im': ssm_c_im,
            'ssm_d': ssm_d, 'w_glu': w_glu, 'b_glu': b_glu,
            'diff_lam_q1': diff_lam_q1, 'diff_lam_k1': diff_lam_k1,
            'diff_lam_q2': diff_lam_q2, 'diff_lam_k2': diff_lam_k2,
            'rel_bias': rel_bias, 'fox_b_f': fox_b_f, 'g_mix_out': g_mix_out, 'w_o': w_o,
            'g_norm_ffn': g_norm_ffn, 'w_ffn_in': w_ffn_in, 'ffn_conv_w': ffn_conv_w,
            'ffn_conv_b': ffn_conv_b, 'w_ffn_out': w_ffn_out, 'g_final': g_final}


def reference(x_prompt, x_sample, cache_diff_k, cache_diff_v, cache_fox_k, cache_fox_v, cache_fox_logf,
              state_ssm_re, state_ssm_im, state_ffn_conv, g_norm_mix, w_in, ssm_lam_re, ssm_lam_im,
              ssm_log_dt, ssm_b_re, ssm_b_im, ssm_c_re, ssm_c_im, ssm_d, w_glu, b_glu,
              diff_lam_q1, diff_lam_k1, diff_lam_q2, diff_lam_k2, rel_bias, fox_b_f, g_mix_out, w_o,
              g_norm_ffn, w_ffn_in, ffn_conv_w, ffn_conv_b, w_ffn_out, g_final):
    f32 = jnp.float32
    per_layer = {'g_norm_mix': g_norm_mix, 'w_in': w_in, 'ssm_lam_re': ssm_lam_re,
                 'ssm_lam_im': ssm_lam_im, 'ssm_log_dt': ssm_log_dt, 'ssm_b_re': ssm_b_re,
                 'ssm_b_im': ssm_b_im, 'ssm_c_re': ssm_c_re, 'ssm_c_im': ssm_c_im, 'ssm_d': ssm_d,
                 'w_glu': w_glu, 'b_glu': b_glu, 'diff_lam_q1': diff_lam_q1, 'diff_lam_k1': diff_lam_k1,
                 'diff_lam_q2': diff_lam_q2, 'diff_lam_k2': diff_lam_k2, 'fox_b_f': fox_b_f,
                 'g_mix_out': g_mix_out, 'w_o': w_o, 'g_norm_ffn': g_norm_ffn, 'w_ffn_in': w_ffn_in,
                 'ffn_conv_w': ffn_conv_w, 'ffn_conv_b': ffn_conv_b, 'w_ffn_out': w_ffn_out}
    bp = x_prompt.shape[0]
    dt = x_prompt.dtype
    st_prompt = (jnp.zeros((bp, 0, H_DIFF, 2, DQK_DIFF), dt), jnp.zeros((bp, 0, H_DIFF, DV_DIFF), dt),
                 jnp.zeros((bp, 0, H_FOX, HD_FOX), dt), jnp.zeros((bp, 0, H_FOX, HD_FOX), dt),
                 jnp.zeros((bp, 0, H_FOX), f32), jnp.zeros((bp, G_SSM, P_SSM), f32),
                 jnp.zeros((bp, G_SSM, P_SSM), f32), jnp.zeros((bp, CONV_W - 1, D_FF), dt))
    xp, xs = x_prompt, x_sample
    outs_p, outs_s = [], []
    for li in range(DEPTH):
        p = {name: arr[li] for name, arr in per_layer.items()}
        p['rel_bias'] = rel_bias
        lam_init = 0.8 - 0.6 * math.exp(-0.3 * li)
        xp, sp = layer(xp, st_prompt, p, lam_init)
        st_sample = (cache_diff_k[li], cache_diff_v[li], cache_fox_k[li], cache_fox_v[li],
                     cache_fox_logf[li], state_ssm_re[li], state_ssm_im[li], state_ffn_conv[li])
        xs, ss = layer(xs, st_sample, p, lam_init)
        outs_p.append(sp)
        outs_s.append(ss)
    (p_diff_k, p_diff_v, p_fox_k, p_fox_v, p_fox_logf, p_ssm_re, p_ssm_im, p_ffn_conv) = [
        jnp.stack([s[i] for s in outs_p]) for i in range(8)]
    (s_diff_k, s_diff_v, s_fox_k, s_fox_v, s_fox_logf, s_ssm_re, s_ssm_im, s_ffn_conv) = [
        jnp.stack([s[i] for s in outs_s]) for i in range(8)]
    y_prompt = rmsnorm(xp, g_final)
    y_sample = rmsnorm(xs, g_final)
    return (y_prompt, y_sample,
            p_diff_k, p_diff_v, p_fox_k, p_fox_v, p_fox_logf, p_ssm_re, p_ssm_im, p_ffn_conv,
            s_diff_k, s_diff_v, s_fox_k, s_fox_v, s_fox_logf, s_ssm_re, s_ssm_im, s_ffn_conv)
```

```python
import functools
import math

import numpy as np
import jax
import jax.numpy as jnp
from jax import lax
from jax.experimental import pallas as pl
from jax.experimental.pallas import tpu as pltpu

F32 = jnp.float32
BF16 = jnp.bfloat16

EPS = 1e-6
CHUNK = 64
SSM_GROUP = 16
P_SSM = 64
DQK_DIFF = 64
DV_DIFF = 128
HD_FOX = 64
N_BUCKETS = 32
MAX_DISTANCE = 128
CONV_W = 3
LANES = 128
KV_BLOCK = 128
FF_BLOCK = 512
ROW_BLOCK = 512
V7X_VMEM_LIMIT = 56 * 1024 * 1024
NEG = -1e30


def _cparams(n_axes):
    return pltpu.CompilerParams(dimension_semantics=("arbitrary",) * n_axes,
                                vmem_limit_bytes=V7X_VMEM_LIMIT)


def _resident(shape):
    nd = len(shape)
    return pl.BlockSpec(shape, lambda *_: (0,) * nd, pipeline_mode=pl.Buffered(1))


def _rms_rows(x):
    return x * lax.rsqrt(jnp.mean(x * x, axis=-1, keepdims=True) + EPS)


def _dot(a, b):
    return jnp.dot(a, b, preferred_element_type=F32)


def _inproj_kernel(x_ref, g_ref, w_ref, *out_refs, cols, scales):
    xn = (_rms_rows(x_ref[...]) * g_ref[...]).astype(BF16)
    for o_ref, (c0, c1), sc in zip(out_refs, cols, scales):
        z = _dot(xn, w_ref[:, c0:c1])
        if sc != 1.0:
            z = z * sc
        o_ref[...] = z.astype(o_ref.dtype)


def _inproj(x, g, w, widths, dtypes, scales):
    n, d = x.shape
    tm = min(ROW_BLOCK, n)
    cols, c = [], 0
    for wd in widths:
        cols.append((c, c + wd))
        c += wd
    return pl.pallas_call(
        functools.partial(_inproj_kernel, cols=tuple(cols), scales=tuple(scales)),
        grid=(n // tm,),
        in_specs=[pl.BlockSpec((tm, d), lambda i: (i, 0)),
                  _resident((1, d)),
                  _resident(w.shape)],
        out_specs=[pl.BlockSpec((tm, wd), lambda i: (i, 0)) for wd in widths],
        out_shape=[jax.ShapeDtypeStruct((n, wd), dt) for wd, dt in zip(widths, dtypes)],
        compiler_params=_cparams(1),
        name="inproj",
    )(x, g, w)


def _s5_kernel(u_ref, kt_ref, p_ref, q_ref, a_ref, d_ref, h0_ref, y_ref, ht_ref,
               m_scr, v_scr, hin_scr, *, nc, bsz):
    width = SSM_GROUP * CHUNK
    lane = lax.broadcasted_iota(jnp.int32, (CHUNK, width), 1)
    row = lax.broadcasted_iota(jnp.int32, (CHUNK, width), 0)
    causal = (lane & (CHUNK - 1)) >= row
    for h in range(SSM_GROUP):
        base = jnp.broadcast_to(kt_ref[0, h:h + 1, :], (CHUNK, width))
        shifted = pltpu.roll(base, 0, 1, stride=1, stride_axis=0)
        m_scr[h * CHUNK:(h + 1) * CHUNK, :] = jnp.where(causal, shifted, 0.0).astype(BF16)

    u = u_ref[0]
    ub = u.astype(BF16)
    y_ref[0] = _dot(ub, m_scr[...]) + d_ref[0] * u
    v_scr[...] = _dot(ub, p_ref[0])

    a_same = a_ref[0, 0:1, :]
    a_cross = a_ref[0, 1:2, :]
    h = h0_ref[0]
    for c in range(nc):
        hin_scr[c * bsz:(c + 1) * bsz, :] = h
        h = a_same * h + a_cross * pltpu.roll(h, P_SSM, 1) + v_scr[c * bsz:(c + 1) * bsz, :]
    ht_ref[0] = h
    y_ref[0] += _dot(hin_scr[...].astype(BF16), q_ref[0])


def _s5_params(lam_re, lam_im, log_dt, b_re, b_im, c_re, c_im, d_skip):
    hp = lax.Precision.HIGHEST
    g = lam_re.shape[0]
    dt = jnp.exp(log_dt)[:, None]
    k = jnp.arange(0, CHUNK + 1, dtype=F32)[:, None, None]
    mag = jnp.exp(lam_re * dt * k)
    pr = mag * jnp.cos(lam_im * dt * k)
    pi = mag * jnp.sin(lam_im * dt * k)
    ar, ai = pr[1], pi[1]
    den = lam_re * lam_re + lam_im * lam_im
    fr = ((ar - 1.0) * lam_re + ai * lam_im) / den
    fi = (ai * lam_re - (ar - 1.0) * lam_im) / den
    bbr = fr[..., None] * b_re - fi[..., None] * b_im
    bbi = fr[..., None] * b_im + fi[..., None] * b_re
    car = c_re[None] * pr[:, :, None, :] - c_im[None] * pi[:, :, None, :]
    cai = c_re[None] * pi[:, :, None, :] + c_im[None] * pr[:, :, None, :]
    kt = (jnp.einsum('kgap,gph->ghak', car[:CHUNK], bbr, precision=hp)
          - jnp.einsum('kgap,gph->ghak', cai[:CHUNK], bbi, precision=hp))
    kt = kt.reshape(g, SSM_GROUP, SSM_GROUP * CHUNK)
    prr = pr[CHUNK - 1::-1][:CHUNK]
    pir = pi[CHUNK - 1::-1][:CHUNK]
    p_re = prr[:, :, :, None] * bbr[None] - pir[:, :, :, None] * bbi[None]
    p_im = prr[:, :, :, None] * bbi[None] + pir[:, :, :, None] * bbr[None]
    pmat = jnp.concatenate([jnp.transpose(p_re, (1, 3, 0, 2)), jnp.transpose(p_im, (1, 3, 0, 2))],
                           axis=-1).reshape(g, SSM_GROUP * CHUNK, 2 * P_SSM)
    q_re = jnp.transpose(car[1:], (1, 3, 2, 0))
    q_im = -jnp.transpose(cai[1:], (1, 3, 2, 0))
    qmat = jnp.concatenate([q_re, q_im], axis=1).reshape(g, 2 * P_SSM, SSM_GROUP * CHUNK)
    a64 = jnp.stack([jnp.concatenate([pr[CHUNK], pr[CHUNK]], -1),
                     jnp.concatenate([-pi[CHUNK], pi[CHUNK]], -1)], axis=1)
    dvec = jnp.repeat(d_skip, CHUNK, axis=-1)[:, None, :]
    return kt, pmat.astype(BF16), qmat.astype(BF16), a64, dvec


def _s5(u, h_re, h_im, sp):
    kt, pmat, qmat, a64, dvec = sp
    bsz, t, dssm = u.shape
    g = dssm // SSM_GROUP
    nc = t // CHUNK
    width = SSM_GROUP * CHUNK
    rows = nc * bsz
    ut = jnp.transpose(u.reshape(bsz, nc, CHUNK, g, SSM_GROUP), (3, 1, 0, 4, 2)).reshape(g, rows, width)
    h0 = jnp.transpose(jnp.concatenate([h_re, h_im], axis=-1), (1, 0, 2))
    y, ht = pl.pallas_call(
        functools.partial(_s5_kernel, nc=nc, bsz=bsz),
        grid=(g,),
        in_specs=[pl.BlockSpec((1, rows, width), lambda i: (i, 0, 0)),
                  pl.BlockSpec((1, SSM_GROUP, width), lambda i: (i, 0, 0)),
                  pl.BlockSpec((1, width, 2 * P_SSM), lambda i: (i, 0, 0)),
                  pl.BlockSpec((1, 2 * P_SSM, width), lambda i: (i, 0, 0)),
                  pl.BlockSpec((1, 2, 2 * P_SSM), lambda i: (i, 0, 0)),
                  pl.BlockSpec((1, 1, width), lambda i: (i, 0, 0)),
                  pl.BlockSpec((1, bsz, 2 * P_SSM), lambda i: (i, 0, 0))],
        out_specs=[pl.BlockSpec((1, rows, width), lambda i: (i, 0, 0)),
                   pl.BlockSpec((1, bsz, 2 * P_SSM), lambda i: (i, 0, 0))],
        out_shape=[jax.ShapeDtypeStruct((g, rows, width), F32),
                   jax.ShapeDtypeStruct((g, bsz, 2 * P_SSM), F32)],
        scratch_shapes=[pltpu.VMEM((width, width), BF16),
                        pltpu.VMEM((rows, 2 * P_SSM), F32),
                        pltpu.VMEM((rows, 2 * P_SSM), F32)],
        compiler_params=_cparams(1),
        name="s5",
    )(ut, kt, pmat, qmat, a64, dvec, h0)
    y = jnp.transpose(y.reshape(g, nc, bsz, SSM_GROUP, CHUNK), (2, 1, 4, 0, 3)).reshape(bsz * t, dssm)
    ht = jnp.transpose(ht, (1, 0, 2))
    return y, ht[..., :P_SSM], ht[..., P_SSM:]


def _glu_kernel(y_ref, w_ref, b_ref, gm_ref, o_ref):
    y = y_ref[...]
    g = 0.5 * y * (1.0 + jnp.tanh(math.sqrt(2.0 / math.pi) * (y + 0.044715 * (y * y * y))))
    z = _dot(g.astype(BF16), w_ref[...]) + b_ref[...]
    yy = g * (1.0 / (1.0 + jnp.exp(-z)))
    o_ref[...] = (_rms_rows(yy) * gm_ref[...]).astype(BF16)


def _glu(y, w, b, gm):
    n, d = y.shape
    tm = min(ROW_BLOCK, n)
    return pl.pallas_call(
        _glu_kernel,
        grid=(n // tm,),
        in_specs=[pl.BlockSpec((tm, d), lambda i: (i, 0)),
                  _resident(w.shape), _resident((1, d)), _resident((1, d))],
        out_specs=pl.BlockSpec((tm, d), lambda i: (i, 0)),
        out_shape=jax.ShapeDtypeStruct((n, d), BF16),
        compiler_params=_cparams(1),
        name="glu",
    )(y, w, b, gm)


def _t5_bucket_np(rel):
    nb = N_BUCKETS // 2
    max_exact = nb // 2
    n = np.abs(rel)
    large = max_exact + (np.log(np.maximum(n, 1).astype(np.float32) / np.float32(max_exact))
                         / np.float32(math.log(MAX_DISTANCE / max_exact))
                         * np.float32(nb - max_exact)).astype(np.int32)
    large = np.minimum(large, nb - 1)
    return (np.where(rel > 0, nb, 0) + np.where(n < max_exact, n, large)).astype(np.int32)


def _online_update(m, l, acc, s, vb):
    m_new = jnp.maximum(m, jnp.max(s, axis=-1, keepdims=True))
    alpha = jnp.exp(m - m_new)
    p = jnp.exp(s - m_new)
    l = alpha * l + jnp.sum(p, axis=-1, keepdims=True)
    acc = alpha * acc + _dot(p.astype(BF16), vb)
    return m_new, l, acc


def _stage_kv(k_ref, v_ref, kp_ref, vp_ref, kt_scr, v_scr, t, past):
    step = 512
    if past:
        for r0 in range(0, past, step):
            r1 = min(r0 + step, past)
            kt_scr[:, r0:r1] = jnp.transpose(kp_ref[0, r0:r1, :]).astype(BF16)
            v_scr[r0:r1, :] = vp_ref[0, r0:r1, :].astype(BF16)
    for r0 in range(0, t, step):
        r1 = min(r0 + step, t)
        kt_scr[:, past + r0:past + r1] = jnp.transpose(k_ref[0, r0:r1, :]).astype(BF16)
        v_scr[past + r0:past + r1, :] = v_ref[0, r0:r1, :].astype(BF16)


def _attn_specs(bsz, t, past, nh_blocks, tq):
    q_spec = pl.BlockSpec((1, tq, LANES), lambda b, h, i: (b, i, h))
    kv_spec = pl.BlockSpec((1, t, LANES), lambda b, h, i: (b, 0, h))
    specs = [q_spec, kv_spec, kv_spec]
    if past:
        p_spec = pl.BlockSpec((1, past, LANES), lambda b, h, i: (b, 0, h))
        specs += [p_spec, p_spec]
    return specs


def _diff_kernel(lam_ref, rb_ref, q_ref, k_ref, v_ref, *rest, t, past, tq, nh, out_scale):
    if past:
        kp_ref, vp_ref = rest[0], rest[1]
        rest = rest[2:]
    else:
        kp_ref = vp_ref = None
    bkn_ref, bkd_ref, gm_ref, o_ref, kt_scr, v_scr, bn_scr, bd_scr = rest
    h = pl.program_id(1)
    qi = pl.program_id(2)
    nq = t // tq

    @pl.when(qi == 0)
    def _():
        _stage_kv(k_ref, v_ref, kp_ref, vp_ref, kt_scr, v_scr, t, past)
        bkn = bkn_ref[...]
        bkd = bkd_ref[...]
        bn = jnp.zeros(bkn.shape, F32)
        bd = jnp.zeros(bkd.shape, F32)
        for b in range(N_BUCKETS):
            val = rb_ref[b * nh + h]
            bn = jnp.where(bkn == b, val, bn)
            bd = jnp.where(bkd == b, val, bd)
        r = lax.broadcasted_iota(jnp.int32, bkd.shape, 0)
        c = lax.broadcasted_iota(jnp.int32, bkd.shape, 1)
        bd = jnp.where((c // CHUNK) <= (r // CHUNK), bd, NEG)
        bn_scr[...] = bn
        bd_scr[...] = bd

    q = q_ref[0]
    lane = lax.broadcasted_iota(jnp.int32, q.shape, 1)
    q1 = jnp.where(lane < DQK_DIFF, q, jnp.zeros_like(q))
    q2 = jnp.where(lane >= DQK_DIFF, q, jnp.zeros_like(q))
    far_bias = rb_ref[(N_BUCKETS // 2 - 1) * nh + h]

    def block(carry, kt, vb, bias):
        m1, l1, a1, m2, l2, a2 = carry
        m1, l1, a1 = _online_update(m1, l1, a1, _dot(q1, kt) + bias, vb)
        m2, l2, a2 = _online_update(m2, l2, a2, _dot(q2, kt) + bias, vb)
        return m1, l1, a1, m2, l2, a2

    def full_block(j, carry, bias):
        off = pl.multiple_of(j * KV_BLOCK, KV_BLOCK)
        return block(carry, kt_scr[:, pl.ds(off, KV_BLOCK)], v_scr[pl.ds(off, KV_BLOCK), :], bias)

    init_col = jnp.full((tq, 1), NEG, F32)
    zero_col = jnp.zeros((tq, 1), F32)
    zero_acc = jnp.zeros((tq, DV_DIFF), F32)
    carry = (init_col, zero_col, zero_acc, init_col, zero_col, zero_acc)

    if nq == 1:
        nfull = past // KV_BLOCK
        if nfull > 1:
            carry = lax.fori_loop(0, nfull - 1, lambda j, cr: full_block(j, cr, far_bias), carry)
        if nfull >= 1:
            carry = full_block(nfull - 1, carry, bn_scr[...])
        d0 = past
    else:
        nfull = (past + qi * tq) // KV_BLOCK
        carry = lax.fori_loop(0, nfull - 1, lambda j, cr: full_block(j, cr, far_bias), carry)
        carry = lax.cond(nfull >= 1,
                         lambda cr: full_block(nfull - 1, cr, bn_scr[...]),
                         lambda cr: cr, carry)
        d0 = pl.multiple_of(past + qi * tq, tq)
    carry = block(carry, kt_scr[:, pl.ds(d0, tq)], v_scr[pl.ds(d0, tq), :], bd_scr[...])

    m1, l1, a1, m2, l2, a2 = carry
    o = a1 / l1 - lam_ref[0] * (a2 / l2)
    o_ref[0] = (_rms_rows(o) * out_scale * gm_ref[...]).astype(BF16)


def _diff_attn(q, k, v, kp, vp, lam, rel_bias, gm, lam_init):
    bsz, t, hw = q.shape
    nh = hw // LANES
    past = 0 if kp is None else kp.shape[1]
    tq = min(KV_BLOCK, t)
    assert t % tq == 0 and past % KV_BLOCK == 0 and (t == tq or tq == KV_BLOCK)
    r = np.arange(tq, dtype=np.int32)[:, None]
    bk_near = _t5_bucket_np(np.arange(KV_BLOCK, dtype=np.int32)[None, :] - KV_BLOCK - r)
    bk_diag = _t5_bucket_np(np.arange(tq, dtype=np.int32)[None, :] - r)
    smem = pl.BlockSpec(memory_space=pltpu.SMEM)
    args = [lam.reshape(1), rel_bias.reshape(-1), q, k, v]
    if past:
        args += [kp, vp]
    args += [jnp.asarray(bk_near), jnp.asarray(bk_diag), gm]
    return pl.pallas_call(
        functools.partial(_diff_kernel, t=t, past=past, tq=tq, nh=nh, out_scale=1.0 - lam_init),
        grid=(bsz, nh, t // tq),
        in_specs=[smem, smem] + _attn_specs(bsz, t, past, nh, tq) + [
            pl.BlockSpec((tq, KV_BLOCK), lambda b, h, i: (0, 0)),
            pl.BlockSpec((tq, tq), lambda b, h, i: (0, 0)),
            pl.BlockSpec((1, LANES), lambda b, h, i: (0, h))],
        out_specs=pl.BlockSpec((1, tq, LANES), lambda b, h, i: (b, i, h)),
        out_shape=jax.ShapeDtypeStruct((bsz, t, hw), BF16),
        scratch_shapes=[pltpu.VMEM((LANES, past + t), BF16),
                        pltpu.VMEM((past + t, LANES), BF16),
                        pltpu.VMEM((tq, KV_BLOCK), F32),
                        pltpu.VMEM((tq, tq), F32)],
        compiler_params=_cparams(3),
        name="diff_attn",
    )(*args)


def _logf_kernel(fg_ref, bf_ref, *rest, t, past, nh):
    if past:
        past_ref, logf_ref, cum_ref = rest
    else:
        logf_ref, cum_ref = rest
    x = fg_ref[0][:, :nh] + bf_ref[...]
    lf = -(jnp.maximum(-x, 0.0) + jnp.log1p(jnp.exp(-jnp.abs(x))))
    logf_ref[0] = lf

    def tri(n):
        r = lax.broadcasted_iota(jnp.int32, (n, n), 0)
        c = lax.broadcasted_iota(jnp.int32, (n, n), 1)
        return (r >= c).astype(F32)

    carry = jnp.zeros((1, nh), F32)
    pos = 0
    for src, length in ((None, past), (lf, t)):
        for r0 in range(0, length, KV_BLOCK):
            n = min(KV_BLOCK, length - r0)
            blk = past_ref[0, r0:r0 + n, :] if src is None else src[r0:r0 + n, :]
            cs = jnp.dot(tri(n), blk, preferred_element_type=F32,
                         precision=lax.Precision.HIGHEST) + carry
            cum_ref[0, pos:pos + n, :] = cs
            carry = cs[n - 1:n, :]
            pos += n


def _logf_cum(fg, b_f, past_logf):
    bsz, t, _ = fg.shape
    nh = b_f.shape[-1]
    past = 0 if past_logf is None else past_logf.shape[1]
    args = [fg, b_f.reshape(1, nh)]
    in_specs = [pl.BlockSpec((1, t, LANES), lambda b: (b, 0, 0)),
                pl.BlockSpec((1, nh), lambda b: (0, 0))]
    if past:
        args.append(past_logf)
        in_specs.append(pl.BlockSpec((1, past, nh), lambda b: (b, 0, 0)))
    return pl.pallas_call(
        functools.partial(_logf_kernel, t=t, past=past, nh=nh),
        grid=(bsz,),
        in_specs=in_specs,
        out_specs=[pl.BlockSpec((1, t, nh), lambda b: (b, 0, 0)),
                   pl.BlockSpec((1, past + t, nh), lambda b: (b, 0, 0))],
        out_shape=[jax.ShapeDtypeStruct((bsz, t, nh), F32),
                   jax.ShapeDtypeStruct((bsz, past + t, nh), F32)],
        compiler_params=_cparams(1),
        name="logf_cum",
    )(*args)


def _fox_kernel(q_ref, k_ref, v_ref, *rest, t, past, tq):
    if past:
        kp_ref, vp_ref = rest[0], rest[1]
        rest = rest[2:]
    else:
        kp_ref = vp_ref = None
    cq_ref, ck_ref, gm_ref, o_ref, kt_scr, v_scr = rest
    hp = pl.program_id(1)
    qi = pl.program_id(2)
    nq = t // tq

    @pl.when(qi == 0)
    def _():
        _stage_kv(k_ref, v_ref, kp_ref, vp_ref, kt_scr, v_scr, t, past)

    q = q_ref[0]
    lane = lax.broadcasted_iota(jnp.int32, q.shape, 1)
    cq_blk = cq_ref[0]
    head_lane = lax.broadcasted_iota(jnp.int32, cq_blk.shape, 1)
    r = lax.broadcasted_iota(jnp.int32, (tq, tq), 0)
    c = lax.broadcasted_iota(jnp.int32, (tq, tq), 1)
    causal = c <= r
    if nq == 1:
        nfull = past // KV_BLOCK
        d0 = past
    else:
        nfull = (past + qi * tq) // KV_BLOCK
        d0 = pl.multiple_of(past + qi * tq, tq)

    def head_row(blk, head):
        rid = lax.broadcasted_iota(jnp.int32, blk.shape, 0)
        return jnp.sum(jnp.where(rid == head, blk, 0.0), axis=0, keepdims=True)

    outs = []
    for hh in range(2):
        head = 2 * hp + hh
        in_half = (lane >= hh * HD_FOX) & (lane < (hh + 1) * HD_FOX)
        qh = jnp.where(in_half, q, jnp.zeros_like(q))
        cq = jnp.sum(jnp.where(head_lane == head, cq_blk, 0.0), axis=-1, keepdims=True)

        def full_block(j, carry, qh=qh, cq=cq, head=head):
            off = pl.multiple_of(j * KV_BLOCK, KV_BLOCK)
            ck = head_row(ck_ref[0, :, pl.ds(off, KV_BLOCK)], head)
            s = _dot(qh, kt_scr[:, pl.ds(off, KV_BLOCK)]) + cq - ck
            return _online_update(*carry, s, v_scr[pl.ds(off, KV_BLOCK), :])

        carry = (jnp.full((tq, 1), NEG, F32), jnp.zeros((tq, 1), F32), jnp.zeros((tq, LANES), F32))
        carry = lax.fori_loop(0, nfull, full_block, carry)
        ck = head_row(ck_ref[0, :, pl.ds(d0, tq)], head)
        s = _dot(qh, kt_scr[:, pl.ds(d0, tq)]) + cq - ck
        s = jnp.where(causal, s, NEG)
        m, l, acc = _online_update(*carry, s, v_scr[pl.ds(d0, tq), :])
        outs.append(acc / l)

    o = jnp.where(lane < HD_FOX, outs[0], outs[1])
    sq = o * o
    ms0 = jnp.sum(jnp.where(lane < HD_FOX, sq, 0.0), axis=-1, keepdims=True) / HD_FOX
    ms1 = jnp.sum(jnp.where(lane >= HD_FOX, sq, 0.0), axis=-1, keepdims=True) / HD_FOX
    inv = jnp.where(lane < HD_FOX, lax.rsqrt(ms0 + EPS), lax.rsqrt(ms1 + EPS))
    o_ref[0] = (o * inv * gm_ref[...]).astype(BF16)


def _fox_attn(q, k, v, kp, vp, cum, gm):
    bsz, t, hw = q.shape
    npair = hw // LANES
    past = 0 if kp is None else kp.shape[1]
    nh = cum.shape[-1]
    tq = min(KV_BLOCK, t)
    assert t % tq == 0 and past % KV_BLOCK == 0 and (t == tq or tq == KV_BLOCK)
    cq = cum[:, past:, :]
    ck = jnp.swapaxes(cum, 1, 2)
    args = [q, k, v]
    if past:
        args += [kp, vp]
    args += [cq, ck, gm]
    return pl.pallas_call(
        functools.partial(_fox_kernel, t=t, past=past, tq=tq),
        grid=(bsz, npair, t // tq),
        in_specs=_attn_specs(bsz, t, past, npair, tq) + [
            pl.BlockSpec((1, tq, nh), lambda b, h, i: (b, i, 0)),
            pl.BlockSpec((1, nh, past + t), lambda b, h, i: (b, 0, 0)),
            pl.BlockSpec((1, LANES), lambda b, h, i: (0, h))],
        out_specs=pl.BlockSpec((1, tq, LANES), lambda b, h, i: (b, i, h)),
        out_shape=jax.ShapeDtypeStruct((bsz, t, hw), BF16),
        scratch_shapes=[pltpu.VMEM((LANES, past + t), BF16),
                        pltpu.VMEM((past + t, LANES), BF16)],
        compiler_params=_cparams(3),
        name="fox_attn",
    )(*args)


def _oproj_kernel(x_ref, a_ref, b_ref, c_ref, w_ref, o_ref, *, d1, d2):
    acc = _dot(a_ref[...], w_ref[0:d1, :])
    acc += _dot(b_ref[...], w_ref[d1:d2, :])
    acc += _dot(c_ref[...], w_ref[d2:, :])
    o_ref[...] = x_ref[...] + acc


def _oproj(x, a, b, c, w):
    n, d = x.shape
    tm = min(ROW_BLOCK, n)
    d1 = a.shape[1]
    d2 = d1 + b.shape[1]
    row = lambda width: pl.BlockSpec((tm, width), lambda i: (i, 0))
    return pl.pallas_call(
        functools.partial(_oproj_kernel, d1=d1, d2=d2),
        grid=(n // tm,),
        in_specs=[row(d), row(a.shape[1]), row(b.shape[1]), row(c.shape[1]), _resident(w.shape)],
        out_specs=row(d),
        out_shape=jax.ShapeDtypeStruct((n, d), F32),
        compiler_params=_cparams(1),
        name="oproj",
    )(x, a, b, c, w)


def _ffn_kernel(x_ref, g_ref, wa_ref, wb_ref, cw_ref, cb_ref, wo_ref, st_ref, o_ref, tail_ref,
                xn_scr, acc_scr, halo_scr, *, tm, t, tf):
    i = pl.program_id(0)
    c = pl.program_id(1)

    @pl.when(c == 0)
    def _():
        xn_scr[...] = (_rms_rows(x_ref[...]) * g_ref[...]).astype(BF16)
        acc_scr[...] = jnp.zeros_like(acc_scr)

    xn = xn_scr[...]
    a = _dot(xn, wa_ref[...])
    b = _dot(xn, wb_ref[...])
    rows = lax.broadcasted_iota(jnp.int32, (tm, tf), 0)
    prev1 = pltpu.roll(a, 1, 0)
    prev2 = pltpu.roll(a, 2, 0)
    tail_rows = 8
    if t >= tm:
        @pl.when((i % (t // tm)) == 0)
        def _():
            halo_scr[c] = st_ref[0]

        src = halo_scr[c]
        h0 = src[tail_rows - 2:tail_rows - 1, :]
        h1 = src[tail_rows - 1:tail_rows, :]
        a1 = jnp.where(rows == 0, h1, prev1)
        a2 = jnp.where(rows == 0, h0, jnp.where(rows == 1, h1, prev2))
        last = a[tm - tail_rows:tm, :]
        halo_scr[c] = last
        tail_ref[0] = last
    else:
        nseq = tm // t
        st = st_ref[...]
        h0 = jnp.broadcast_to(st[:, tail_rows - 2:tail_rows - 1, :], (nseq, t, tf)).reshape(tm, tf)
        h1 = jnp.broadcast_to(st[:, tail_rows - 1:tail_rows, :], (nseq, t, tf)).reshape(tm, tf)
        rmod = rows & (t - 1)
        a1 = jnp.where(rmod == 0, h1, prev1)
        a2 = jnp.where(rmod == 0, h0, jnp.where(rmod == 1, h1, prev2))
        tail_ref[...] = a.reshape(nseq, t, tf)[:, t - tail_rows:, :]
    ac = cw_ref[2:3, :] * a + cw_ref[1:2, :] * a1 + cw_ref[0:1, :] * a2 + cb_ref[...]
    hh = ac * (1.0 / (1.0 + jnp.exp(-ac))) * b
    acc_scr[...] += _dot(hh.astype(BF16), wo_ref[...])

    @pl.when(c == pl.num_programs(1) - 1)
    def _():
        o_ref[...] = x_ref[...] + acc_scr[...]


def _ffn(x, t, g, wa, wb, cw, cb, wo, st):
    n, d = x.shape
    ffp = wa.shape[1]
    tf = min(FF_BLOCK, ffp)
    tm = min(ROW_BLOCK, n)
    assert ffp % tf == 0 and n % tm == 0 and (t % tm == 0 or tm % t == 0)
    assert t & (t - 1) == 0
    nseq = n // t
    if t >= tm:
        per = t // tm
        st_spec = pl.BlockSpec((1, 8, tf), lambda i, c: (i // per, 0, c))
        tail_spec = pl.BlockSpec((1, 8, tf), lambda i, c: (i, 0, c))
    else:
        per = 1
        st_spec = tail_spec = pl.BlockSpec((tm // t, 8, tf), lambda i, c: (i, 0, c))
    x_new, tail = pl.pallas_call(
        functools.partial(_ffn_kernel, tm=tm, t=t, tf=tf),
        grid=(n // tm, ffp // tf),
        in_specs=[pl.BlockSpec((tm, d), lambda i, c: (i, 0)),
                  pl.BlockSpec((1, d), lambda i, c: (0, 0)),
                  pl.BlockSpec((d, tf), lambda i, c: (0, c)),
                  pl.BlockSpec((d, tf), lambda i, c: (0, c)),
                  pl.BlockSpec((8, tf), lambda i, c: (0, c)),
                  pl.BlockSpec((1, tf), lambda i, c: (0, c)),
                  pl.BlockSpec((tf, d), lambda i, c: (c, 0)),
                  st_spec],
        out_specs=[pl.BlockSpec((tm, d), lambda i, c: (i, 0)), tail_spec],
        out_shape=[jax.ShapeDtypeStruct((n, d), F32),
                   jax.ShapeDtypeStruct((nseq * per, 8, ffp), F32)],
        scratch_shapes=[pltpu.VMEM((tm, d), BF16),
                        pltpu.VMEM((tm, d), F32),
                        pltpu.VMEM((ffp // tf, 8, tf), F32)],
        compiler_params=_cparams(2),
        name="ffn",
    )(x, g, wa, wb, cw, cb, wo, st)
    return x_new, tail.reshape(nseq, per, 8, ffp)[:, per - 1]


def _norm_kernel(x_ref, g_ref, o_ref):
    o_ref[...] = _rms_rows(x_ref[...]) * g_ref[...]


def _final_norm(x, g):
    n, d = x.shape
    tm = min(ROW_BLOCK, n)
    return pl.pallas_call(
        _norm_kernel,
        grid=(n // tm,),
        in_specs=[pl.BlockSpec((tm, d), lambda i: (i, 0)), pl.BlockSpec((1, d), lambda i: (0, 0))],
        out_specs=pl.BlockSpec((tm, d), lambda i: (i, 0)),
        out_shape=jax.ShapeDtypeStruct((n, d), F32),
        compiler_params=_cparams(1),
        name="final_norm",
    )(x, g)


def _prep_layer(p, dims):
    d_ssm, hd, hf, d_ff, ffp = dims
    off_fg = d_ssm + 3 * hd * LANES + 3 * hf * HD_FOX
    w_in = p['w_in']
    w_fg = jnp.pad(w_in[:, off_fg:], ((0, 0), (0, LANES - hf)))
    q = dict(p)
    q['w_in'] = jnp.concatenate([w_in[:, :off_fg], w_fg], axis=1).astype(BF16)
    q['w_glu'] = p['w_glu'].astype(BF16)
    q['w_o'] = p['w_o'].astype(BF16)
    padc = ((0, 0), (0, ffp - d_ff))
    q['w_a'] = jnp.pad(p['w_ffn_in'][:, :d_ff], padc).astype(BF16)
    q['w_b'] = jnp.pad(p['w_ffn_in'][:, d_ff:], padc).astype(BF16)
    q['w_ffn_out'] = jnp.pad(p['w_ffn_out'], ((0, ffp - d_ff), (0, 0))).astype(BF16)
    q['conv_w'] = jnp.pad(p['ffn_conv_w'], ((0, 8 - CONV_W), (0, ffp - d_ff)))
    q['conv_b'] = jnp.pad(p['ffn_conv_b'], (0, ffp - d_ff))[None, :]
    q['s5'] = _s5_params(p['ssm_lam_re'], p['ssm_lam_im'], p['ssm_log_dt'], p['ssm_b_re'],
                         p['ssm_b_im'], p['ssm_c_re'], p['ssm_c_im'], p['ssm_d'])
    q['lam'] = (jnp.exp(jnp.sum(p['diff_lam_q1'] * p['diff_lam_k1']))
                - jnp.exp(jnp.sum(p['diff_lam_q2'] * p['diff_lam_k2'])) + p['lam_init'])
    return q


def _layer(x, bsz, t, st, p, dims, lam_init):
    d_ssm, hd, hf, d_ff, ffp = dims
    kd_c, vd_c, kf_c, vf_c, lf_c, sr_c, si_c, cv_c = st
    n, d = x.shape
    wd, wf = hd * LANES, hf * HD_FOX
    widths = [d_ssm, wd, wd, wd, wf, wf, wf, LANES]
    dtypes = [F32, BF16, F32, F32, BF16, F32, F32, F32]
    scales = [1.0, DQK_DIFF ** -0.5, 1.0, 1.0, HD_FOX ** -0.5, 1.0, 1.0, 1.0]
    u, qd, kd, vd, qf, kf, vf, fg = _inproj(x, p['g_norm_mix'][None, :], p['w_in'], widths, dtypes, scales)

    gm = p['g_mix_out'][None, :]
    y_ssm, sr, si = _s5(u.reshape(bsz, t, d_ssm), sr_c, si_c, p['s5'])
    o_ssm = _glu(y_ssm, p['w_glu'], p['b_glu'][None, :], gm[:, :d_ssm])

    r3 = lambda a: a.reshape(bsz, t, a.shape[-1])
    hist = lambda a: None if a is None else a.reshape(a.shape[0], a.shape[1], -1)
    o_diff = _diff_attn(r3(qd), r3(kd), r3(vd), hist(kd_c), hist(vd_c), p['lam'], p['rel_bias'],
                        gm[:, d_ssm:d_ssm + wd], lam_init)

    logf, cum = _logf_cum(r3(fg), p['fox_b_f'], lf_c)
    o_fox = _fox_attn(r3(qf), r3(kf), r3(vf), hist(kf_c), hist(vf_c), cum, gm[:, d_ssm + wd:])

    x = _oproj(x, o_ssm, o_diff.reshape(n, wd), o_fox.reshape(n, wf), p['w_o'])

    x, tail = _ffn(x, t, p['g_norm_ffn'][None, :], p['w_a'], p['w_b'], p['conv_w'], p['conv_b'],
                   p['w_ffn_out'], cv_c)
    new_st = (kd.reshape(bsz, t, hd, 2, DQK_DIFF), vd.reshape(bsz, t, hd, DV_DIFF),
              kf.reshape(bsz, t, hf, HD_FOX), vf.reshape(bsz, t, hf, HD_FOX), logf, sr, si,
              tail[:, 8 - (CONV_W - 1):, :d_ff])
    return x, new_st


def kernel(x_prompt, x_sample, cache_diff_k, cache_diff_v, cache_fox_k, cache_fox_v, cache_fox_logf, state_ssm_re, state_ssm_im, state_ffn_conv, g_norm_mix, w_in, ssm_lam_re, ssm_lam_im, ssm_log_dt, ssm_b_re, ssm_b_im, ssm_c_re, ssm_c_im, ssm_d, w_glu, b_glu, diff_lam_q1, diff_lam_k1, diff_lam_q2, diff_lam_k2, rel_bias, fox_b_f, g_mix_out, w_o, g_norm_ffn, w_ffn_in, ffn_conv_w, ffn_conv_b, w_ffn_out, g_final):
    per_layer = {'g_norm_mix': g_norm_mix, 'w_in': w_in, 'ssm_lam_re': ssm_lam_re,
                 'ssm_lam_im': ssm_lam_im, 'ssm_log_dt': ssm_log_dt, 'ssm_b_re': ssm_b_re,
                 'ssm_b_im': ssm_b_im, 'ssm_c_re': ssm_c_re, 'ssm_c_im': ssm_c_im, 'ssm_d': ssm_d,
                 'w_glu': w_glu, 'b_glu': b_glu, 'diff_lam_q1': diff_lam_q1, 'diff_lam_k1': diff_lam_k1,
                 'diff_lam_q2': diff_lam_q2, 'diff_lam_k2': diff_lam_k2, 'fox_b_f': fox_b_f,
                 'g_mix_out': g_mix_out, 'w_o': w_o, 'g_norm_ffn': g_norm_ffn, 'w_ffn_in': w_ffn_in,
                 'ffn_conv_w': ffn_conv_w, 'ffn_conv_b': ffn_conv_b, 'w_ffn_out': w_ffn_out}
    depth = w_in.shape[0]
    bp, tp, d_model = x_prompt.shape
    bs, ts, _ = x_sample.shape
    g_ssm = ssm_lam_re.shape[1]
    d_ssm = g_ssm * SSM_GROUP
    hd = cache_diff_k.shape[3]
    hf = cache_fox_k.shape[3]
    d_ff = w_ffn_out.shape[1]
    ffp = -(-d_ff // FF_BLOCK) * FF_BLOCK if d_ff > FF_BLOCK else d_ff
    dims = (d_ssm, hd, hf, d_ff, ffp)

    def conv_state_block(s):
        return jnp.pad(s, ((0, 0), (8 - (CONV_W - 1), 0), (0, ffp - d_ff)))

    st_prompt = (None, None, None, None, None,
                 jnp.zeros((bp, g_ssm, P_SSM), F32), jnp.zeros((bp, g_ssm, P_SSM), F32),
                 conv_state_block(jnp.zeros((bp, CONV_W - 1, d_ff), F32)))
    xp = x_prompt.reshape(bp * tp, d_model)
    xs = x_sample.reshape(bs * ts, d_model)
    outs_p, outs_s = [], []
    for li in range(depth):
        lam_init = 0.8 - 0.6 * math.exp(-0.3 * li)
        p = {name: arr[li] for name, arr in per_layer.items()}
        p['rel_bias'] = rel_bias
        p['lam_init'] = lam_init
        p = _prep_layer(p, dims)
        xp, sp = _layer(xp, bp, tp, st_prompt, p, dims, lam_init)
        st_sample = (cache_diff_k[li], cache_diff_v[li], cache_fox_k[li], cache_fox_v[li],
                     cache_fox_logf[li], state_ssm_re[li], state_ssm_im[li],
                     conv_state_block(state_ffn_conv[li]))
        xs, ss = _layer(xs, bs, ts, st_sample, p, dims, lam_init)
        outs_p.append(sp)
        outs_s.append(ss)
    stacked_p = [jnp.stack([s[i] for s in outs_p]) for i in range(8)]
    stacked_s = [jnp.stack([s[i] for s in outs_s]) for i in range(8)]
    y_prompt = _final_norm(xp, g_final[None, :]).reshape(bp, tp, d_model)
    y_sample = _final_norm(xs, g_final[None, :]).reshape(bs, ts, d_model)
    return (y_prompt, y_sample, *stacked_p, *stacked_s)
```

```python
import functools
import math

import numpy as np
import jax
import jax.numpy as jnp
from jax import lax
from jax.experimental import pallas as pl
from jax.experimental.pallas import tpu as pltpu

F32 = jnp.float32
BF16 = jnp.bfloat16

EPS = 1e-6
CHUNK = 64
SSM_GROUP = 16
P_SSM = 64
DQK_DIFF = 64
DV_DIFF = 128
HD_FOX = 64
N_BUCKETS = 32
MAX_DISTANCE = 128
CONV_W = 3
LANES = 128
ATTN_BLOCK = 512
FF_BLOCK = 512
ROW_BLOCK = 512
V7X_VMEM_LIMIT = 56 * 1024 * 1024
NEG = -1e30


def _cparams(n_axes):
    return pltpu.CompilerParams(dimension_semantics=("arbitrary",) * n_axes,
                                vmem_limit_bytes=V7X_VMEM_LIMIT)


def _resident(shape):
    nd = len(shape)
    return pl.BlockSpec(shape, lambda *_: (0,) * nd, pipeline_mode=pl.Buffered(1))


def _rms_rows(x):
    return x * lax.rsqrt(jnp.mean(x * x, axis=-1, keepdims=True) + EPS)


def _dot(a, b):
    return jnp.dot(a, b, preferred_element_type=F32)


def _dot_nt(a, b):
    return lax.dot_general(a, b, (((1,), (1,)), ((), ())), preferred_element_type=F32)


def _inproj_kernel(x_ref, g_ref, wn_ref, wt_ref, u_ref, qd_ref, qf_ref, fg_ref, vd_ref,
                   kdt_ref, kft_ref, vft_ref, *, d_ssm, wd, wf, ns):
    tm = x_ref.shape[0]
    xn = (_rms_rows(x_ref[...]) * g_ref[...]).astype(BF16)
    c = 0
    u_ref[...] = _dot(xn, wn_ref[:, c:c + d_ssm])
    c += d_ssm
    qd_ref[...] = (_dot(xn, wn_ref[:, c:c + wd]) * DQK_DIFF ** -0.5).astype(BF16)
    c += wd
    qf_ref[...] = (_dot(xn, wn_ref[:, c:c + wf]) * HD_FOX ** -0.5).astype(BF16)
    c += wf
    fg_ref[...] = _dot(xn, wn_ref[:, c:c + LANES])
    c += LANES
    vd = _dot(xn, wn_ref[:, c:c + wd])
    nh = wd // DV_DIFF
    for h in range(nh):
        vd_ref[pl.ds(h, tm, stride=nh), :] = vd[:, h * DV_DIFF:(h + 1) * DV_DIFF]
    r = 0
    ts = tm // ns
    for o_ref, width in ((kdt_ref, wd), (kft_ref, wf), (vft_ref, wf)):
        zt = _dot_nt(wt_ref[r:r + width, :], xn)
        for s in range(ns):
            o_ref[s] = zt[:, s * ts:(s + 1) * ts]
        r += width


def _inproj(x, t, g, wn, wt, d_ssm, wd, wf):
    n, d = x.shape
    tm = min(ROW_BLOCK, n)
    nseq = n // t
    nh = wd // DV_DIFF
    assert n % tm == 0 and (t % tm == 0 or tm % t == 0)
    if t >= tm:
        per, ns = t // tm, 1
        tspec = lambda width: pl.BlockSpec((1, width, tm), lambda i: (i // per, 0, i % per))
    else:
        ns = tm // t
        tspec = lambda width: pl.BlockSpec((ns, width, t), lambda i: (i, 0, 0))
    row = lambda width: pl.BlockSpec((tm, width), lambda i: (i, 0))
    return pl.pallas_call(
        functools.partial(_inproj_kernel, d_ssm=d_ssm, wd=wd, wf=wf, ns=ns),
        grid=(n // tm,),
        in_specs=[row(d), _resident((1, d)), _resident(wn.shape), _resident(wt.shape)],
        out_specs=[row(d_ssm), row(wd), row(wf), row(LANES),
                   pl.BlockSpec((tm * nh, DV_DIFF), lambda i: (i, 0)),
                   tspec(wd), tspec(wf), tspec(wf)],
        out_shape=[jax.ShapeDtypeStruct((n, d_ssm), F32),
                   jax.ShapeDtypeStruct((n, wd), BF16),
                   jax.ShapeDtypeStruct((n, wf), BF16),
                   jax.ShapeDtypeStruct((n, LANES), F32),
                   jax.ShapeDtypeStruct((n * nh, DV_DIFF), F32),
                   jax.ShapeDtypeStruct((nseq, wd, t), F32),
                   jax.ShapeDtypeStruct((nseq, wf, t), F32),
                   jax.ShapeDtypeStruct((nseq, wf, t), F32)],
        compiler_params=_cparams(1),
        name="inproj",
    )(x, g, wn, wt)


def _s5_kernel(u_ref, kt_ref, p_ref, q_ref, a_ref, d_ref, h0_ref, y_ref, ht_ref,
               m_scr, v_scr, hin_scr, *, nc, bsz):
    width = SSM_GROUP * CHUNK
    lane = lax.broadcasted_iota(jnp.int32, (CHUNK, width), 1)
    row = lax.broadcasted_iota(jnp.int32, (CHUNK, width), 0)
    causal = (lane & (CHUNK - 1)) >= row
    for h in range(SSM_GROUP):
        base = jnp.broadcast_to(kt_ref[0, h:h + 1, :], (CHUNK, width))
        shifted = pltpu.roll(base, 0, 1, stride=1, stride_axis=0)
        m_scr[h * CHUNK:(h + 1) * CHUNK, :] = jnp.where(causal, shifted, 0.0).astype(BF16)

    u = u_ref[0]
    ub = u.astype(BF16)
    y_ref[0] = _dot(ub, m_scr[...]) + d_ref[0] * u
    v_scr[...] = _dot(ub, p_ref[0])

    a_same = a_ref[0, 0:1, :]
    a_cross = a_ref[0, 1:2, :]
    h = h0_ref[0]
    for c in range(nc):
        hin_scr[c * bsz:(c + 1) * bsz, :] = h
        h = a_same * h + a_cross * pltpu.roll(h, P_SSM, 1) + v_scr[c * bsz:(c + 1) * bsz, :]
    ht_ref[0] = h
    y_ref[0] += _dot(hin_scr[...].astype(BF16), q_ref[0])


def _s5_params(lam_re, lam_im, log_dt, b_re, b_im, c_re, c_im, d_skip):
    hp = lax.Precision.HIGHEST
    g = lam_re.shape[0]
    dt = jnp.exp(log_dt)[:, None]
    k = jnp.arange(0, CHUNK + 1, dtype=F32)[:, None, None]
    mag = jnp.exp(lam_re * dt * k)
    pr = mag * jnp.cos(lam_im * dt * k)
    pi = mag * jnp.sin(lam_im * dt * k)
    ar, ai = pr[1], pi[1]
    den = lam_re * lam_re + lam_im * lam_im
    fr = ((ar - 1.0) * lam_re + ai * lam_im) / den
    fi = (ai * lam_re - (ar - 1.0) * lam_im) / den
    bbr = fr[..., None] * b_re - fi[..., None] * b_im
    bbi = fr[..., None] * b_im + fi[..., None] * b_re
    car = c_re[None] * pr[:, :, None, :] - c_im[None] * pi[:, :, None, :]
    cai = c_re[None] * pi[:, :, None, :] + c_im[None] * pr[:, :, None, :]
    kt = (jnp.einsum('kgap,gph->ghak', car[:CHUNK], bbr, precision=hp)
          - jnp.einsum('kgap,gph->ghak', cai[:CHUNK], bbi, precision=hp))
    kt = kt.reshape(g, SSM_GROUP, SSM_GROUP * CHUNK)
    prr = pr[CHUNK - 1::-1][:CHUNK]
    pir = pi[CHUNK - 1::-1][:CHUNK]
    p_re = prr[:, :, :, None] * bbr[None] - pir[:, :, :, None] * bbi[None]
    p_im = prr[:, :, :, None] * bbi[None] + pir[:, :, :, None] * bbr[None]
    pmat = jnp.concatenate([jnp.transpose(p_re, (1, 3, 0, 2)), jnp.transpose(p_im, (1, 3, 0, 2))],
                           axis=-1).reshape(g, SSM_GROUP * CHUNK, 2 * P_SSM)
    q_re = jnp.transpose(car[1:], (1, 3, 2, 0))
    q_im = -jnp.transpose(cai[1:], (1, 3, 2, 0))
    qmat = jnp.concatenate([q_re, q_im], axis=1).reshape(g, 2 * P_SSM, SSM_GROUP * CHUNK)
    a64 = jnp.stack([jnp.concatenate([pr[CHUNK], pr[CHUNK]], -1),
                     jnp.concatenate([-pi[CHUNK], pi[CHUNK]], -1)], axis=1)
    dvec = jnp.repeat(d_skip, CHUNK, axis=-1)[:, None, :]
    return kt, pmat.astype(BF16), qmat.astype(BF16), a64, dvec


def _s5(u, h_re, h_im, sp):
    kt, pmat, qmat, a64, dvec = sp
    bsz, t, dssm = u.shape
    g = dssm // SSM_GROUP
    nc = t // CHUNK
    width = SSM_GROUP * CHUNK
    rows = nc * bsz
    ut = jnp.transpose(u.reshape(bsz, nc, CHUNK, g, SSM_GROUP), (3, 1, 0, 4, 2)).reshape(g, rows, width)
    h0 = jnp.transpose(jnp.concatenate([h_re, h_im], axis=-1), (1, 0, 2))
    y, ht = pl.pallas_call(
        functools.partial(_s5_kernel, nc=nc, bsz=bsz),
        grid=(g,),
        in_specs=[pl.BlockSpec((1, rows, width), lambda i: (i, 0, 0)),
                  pl.BlockSpec((1, SSM_GROUP, width), lambda i: (i, 0, 0)),
                  pl.BlockSpec((1, width, 2 * P_SSM), lambda i: (i, 0, 0)),
                  pl.BlockSpec((1, 2 * P_SSM, width), lambda i: (i, 0, 0)),
                  pl.BlockSpec((1, 2, 2 * P_SSM), lambda i: (i, 0, 0)),
                  pl.BlockSpec((1, 1, width), lambda i: (i, 0, 0)),
                  pl.BlockSpec((1, bsz, 2 * P_SSM), lambda i: (i, 0, 0))],
        out_specs=[pl.BlockSpec((1, rows, width), lambda i: (i, 0, 0)),
                   pl.BlockSpec((1, bsz, 2 * P_SSM), lambda i: (i, 0, 0))],
        out_shape=[jax.ShapeDtypeStruct((g, rows, width), F32),
                   jax.ShapeDtypeStruct((g, bsz, 2 * P_SSM), F32)],
        scratch_shapes=[pltpu.VMEM((width, width), BF16),
                        pltpu.VMEM((rows, 2 * P_SSM), F32),
                        pltpu.VMEM((rows, 2 * P_SSM), F32)],
        compiler_params=_cparams(1),
        name="s5",
    )(ut, kt, pmat, qmat, a64, dvec, h0)
    y = jnp.transpose(y.reshape(g, nc, bsz, SSM_GROUP, CHUNK), (2, 1, 4, 0, 3)).reshape(bsz * t, dssm)
    ht = jnp.transpose(ht, (1, 0, 2))
    return y, ht[..., :P_SSM], ht[..., P_SSM:]


def _glu_kernel(y_ref, w_ref, b_ref, gm_ref, o_ref):
    y = y_ref[...]
    g = 0.5 * y * (1.0 + jnp.tanh(math.sqrt(2.0 / math.pi) * (y + 0.044715 * (y * y * y))))
    z = _dot(g.astype(BF16), w_ref[...]) + b_ref[...]
    yy = g * (1.0 / (1.0 + jnp.exp(-z)))
    o_ref[...] = (_rms_rows(yy) * gm_ref[...]).astype(BF16)


def _glu(y, w, b, gm):
    n, d = y.shape
    tm = min(ROW_BLOCK, n)
    return pl.pallas_call(
        _glu_kernel,
        grid=(n // tm,),
        in_specs=[pl.BlockSpec((tm, d), lambda i: (i, 0)),
                  _resident(w.shape), _resident((1, d)), _resident((1, d))],
        out_specs=pl.BlockSpec((tm, d), lambda i: (i, 0)),
        out_shape=jax.ShapeDtypeStruct((n, d), BF16),
        compiler_params=_cparams(1),
        name="glu",
    )(y, w, b, gm)


def _t5_bucket_np(rel):
    nb = N_BUCKETS // 2
    max_exact = nb // 2
    n = np.abs(rel)
    large = max_exact + (np.log(np.maximum(n, 1).astype(np.float32) / np.float32(max_exact))
                         / np.float32(math.log(MAX_DISTANCE / max_exact))
                         * np.float32(nb - max_exact)).astype(np.int32)
    large = np.minimum(large, nb - 1)
    return (np.where(rel > 0, nb, 0) + np.where(n < max_exact, n, large)).astype(np.int32)


def _bucket_thresholds():
    dist = np.arange(0, 4 * MAX_DISTANCE, dtype=np.int32)
    bk = _t5_bucket_np(-dist)
    assert np.all(np.diff(bk) >= 0) and bk[-1] == N_BUCKETS // 2 - 1
    assert np.array_equal(_t5_bucket_np(dist[1:]), bk[1:] + N_BUCKETS // 2)
    return [int(np.argmax(bk >= k)) for k in range(N_BUCKETS // 2)]


def _attn_blocks(t, past):
    tq = min(ATTN_BLOCK, t)
    nq = t // tq
    assert t % tq == 0 and tq % CHUNK == 0
    if nq == 1:
        tk = past if past else tq
    else:
        assert past == 0
        tk = tq
    return tq, tk, nq


def _online_update(m, l, acc, s, pv):
    m_new = jnp.maximum(m, jnp.max(s, axis=-1, keepdims=True))
    alpha = jnp.exp(m - m_new)
    p = jnp.exp(s - m_new)
    l = alpha * l + jnp.sum(p, axis=-1, keepdims=True)
    acc = alpha * acc + pv(p.astype(BF16))
    return m_new, l, acc


def _softmax_init(tq, width):
    return (jnp.full((tq, 1), NEG, F32), jnp.zeros((tq, 1), F32), jnp.zeros((tq, width), F32))


def _bias_kernel(rb_ref, near_ref, diag_ref, *, nh, thr):
    h = pl.program_id(0)
    half = N_BUCKETS // 2

    def tile(shape, col_off):
        r = lax.broadcasted_iota(jnp.int32, shape, 0)
        c = lax.broadcasted_iota(jnp.int32, shape, 1)
        rel = c + col_off - r
        dist = jnp.abs(rel)
        back = jnp.full(shape, rb_ref[h], F32)
        fwd = jnp.full(shape, rb_ref[half * nh + h], F32)
        for k in range(1, half):
            far = dist >= thr[k]
            back = jnp.where(far, rb_ref[k * nh + h], back)
            fwd = jnp.where(far, rb_ref[(half + k) * nh + h], fwd)
        return jnp.where(rel > 0, fwd, back), r, c

    near, _, _ = tile(near_ref.shape[1:], -near_ref.shape[2])
    near_ref[0] = near
    diag, r, c = tile(diag_ref.shape[1:], 0)
    diag_ref[0] = jnp.where((c // CHUNK) <= (r // CHUNK), diag, NEG)


def _bias_tiles(rel_bias, tq, tk):
    nh = rel_bias.shape[1]
    return pl.pallas_call(
        functools.partial(_bias_kernel, nh=nh, thr=_bucket_thresholds()),
        grid=(nh,),
        in_specs=[pl.BlockSpec(memory_space=pltpu.SMEM)],
        out_specs=[pl.BlockSpec((1, tq, tk), lambda h: (h, 0, 0)),
                   pl.BlockSpec((1, tq, tq), lambda h: (h, 0, 0))],
        out_shape=[jax.ShapeDtypeStruct((nh, tq, tk), F32),
                   jax.ShapeDtypeStruct((nh, tq, tq), F32)],
        compiler_params=_cparams(1),
        name="rel_bias_tiles",
    )(rel_bias.reshape(-1))


def _diff_kernel(lam_ref, rb_ref, q_ref, kt_ref, v_ref, *rest, t, past, tq, tk, nh, out_scale):
    if past:
        kth_ref, vh_ref = rest[0], rest[1]
        rest = rest[2:]
    bn_ref, bd_ref, gm_ref, o_ref, kt_scr, v_scr = rest
    h = pl.program_id(1)
    qi = pl.program_id(2)
    nq = t // tq

    @pl.when(qi == 0)
    def _():
        if past:
            kt_scr[:, 0:past] = kth_ref[0, 0].astype(BF16)
            v_scr[0:past, :] = vh_ref[0, 0, pl.ds(h, past, stride=nh), :].astype(BF16)
        kt_scr[:, past:past + t] = kt_ref[0].astype(BF16)
        v_scr[past:past + t, :] = v_ref[0, pl.ds(h, t, stride=nh), :].astype(BF16)

    q = q_ref[0]
    lane = lax.broadcasted_iota(jnp.int32, q.shape, 1)
    q1 = jnp.where(lane < DQK_DIFF, q, jnp.zeros_like(q))
    q2 = jnp.where(lane >= DQK_DIFF, q, jnp.zeros_like(q))
    far_bias = rb_ref[(N_BUCKETS // 2 - 1) * nh + h]

    def block(carry, kt, vb, bias):
        c1, c2 = carry
        pv = lambda p: _dot(p, vb)
        c1 = _online_update(*c1, _dot(q1, kt) + bias, pv)
        c2 = _online_update(*c2, _dot(q2, kt) + bias, pv)
        return c1, c2

    carry = (_softmax_init(tq, DV_DIFF), _softmax_init(tq, DV_DIFF))
    if nq == 1:
        if past:
            carry = block(carry, kt_scr[:, 0:past], v_scr[0:past, :], bn_ref[0])
        d0 = past
    else:
        def full_block(j, cr):
            off = pl.multiple_of(j * tk, tk)
            bias = jnp.where(j == qi - 1, bn_ref[0], far_bias)
            return block(cr, kt_scr[:, pl.ds(off, tk)], v_scr[pl.ds(off, tk), :], bias)

        carry = lax.fori_loop(0, qi, full_block, carry)
        d0 = pl.multiple_of(qi * tq, tq)
    carry = block(carry, kt_scr[:, pl.ds(d0, tq)], v_scr[pl.ds(d0, tq), :], bd_ref[0])

    (_, l1, a1), (_, l2, a2) = carry
    o = a1 / l1 - lam_ref[0] * (a2 / l2)
    o_ref[0] = (_rms_rows(o) * out_scale * gm_ref[...]).astype(BF16)


def _diff_attn(q, kt, v, kth, vh, li, lam, rel_bias, bias_tiles, gm, lam_init):
    bsz, t, hw = q.shape
    nh = hw // LANES
    past = 0 if kth is None else kth.shape[-1]
    tq, tk, nq = _attn_blocks(t, past)
    bias_near, bias_diag = bias_tiles
    smem = pl.BlockSpec(memory_space=pltpu.SMEM)
    args = [lam.reshape(1), rel_bias.reshape(-1), q, kt, v]
    in_specs = [smem, smem,
                pl.BlockSpec((1, tq, LANES), lambda b, h, i: (b, i, h)),
                pl.BlockSpec((1, LANES, t), lambda b, h, i: (b, h, 0)),
                pl.BlockSpec((1, t * nh, LANES), lambda b, h, i: (b, 0, 0))]
    if past:
        args += [kth, vh]
        in_specs += [pl.BlockSpec((1, 1, LANES, past), lambda b, h, i: (li, b, h, 0)),
                     pl.BlockSpec((1, 1, past * nh, LANES), lambda b, h, i: (li, b, 0, 0))]
    args += [bias_near, bias_diag, gm]
    in_specs += [pl.BlockSpec((1, tq, tk), lambda b, h, i: (h, 0, 0)),
                 pl.BlockSpec((1, tq, tq), lambda b, h, i: (h, 0, 0)),
                 pl.BlockSpec((1, LANES), lambda b, h, i: (0, h))]
    return pl.pallas_call(
        functools.partial(_diff_kernel, t=t, past=past, tq=tq, tk=tk, nh=nh, out_scale=1.0 - lam_init),
        grid=(bsz, nh, nq),
        in_specs=in_specs,
        out_specs=pl.BlockSpec((1, tq, LANES), lambda b, h, i: (b, i, h)),
        out_shape=jax.ShapeDtypeStruct((bsz, t, hw), BF16),
        scratch_shapes=[pltpu.VMEM((LANES, past + t), BF16),
                        pltpu.VMEM((past + t, LANES), BF16)],
        compiler_params=_cparams(3),
        name="diff_attn",
    )(*args)


def _logf_kernel(fg_ref, bf_ref, *rest, t, past, nh):
    if past:
        past_ref, logf_ref, cum_ref = rest
    else:
        logf_ref, cum_ref = rest
    x = fg_ref[0][:, :nh] + bf_ref[...]
    lf = -(jnp.maximum(-x, 0.0) + jnp.log1p(jnp.exp(-jnp.abs(x))))
    logf_ref[0] = lf

    def tri(n):
        r = lax.broadcasted_iota(jnp.int32, (n, n), 0)
        c = lax.broadcasted_iota(jnp.int32, (n, n), 1)
        return (r >= c).astype(F32)

    carry = jnp.zeros((1, nh), F32)
    pos = 0
    for src, length in ((None, past), (lf, t)):
        for r0 in range(0, length, LANES):
            n = min(LANES, length - r0)
            blk = past_ref[0, r0:r0 + n, :] if src is None else src[r0:r0 + n, :]
            cs = jnp.dot(tri(n), blk, preferred_element_type=F32,
                         precision=lax.Precision.HIGHEST) + carry
            cum_ref[0, pos:pos + n, :] = cs
            carry = cs[n - 1:n, :]
            pos += n


def _logf_cum(fg, b_f, past_logf):
    bsz, t, _ = fg.shape
    nh = b_f.shape[-1]
    past = 0 if past_logf is None else past_logf.shape[1]
    args = [fg, b_f.reshape(1, nh)]
    in_specs = [pl.BlockSpec((1, t, LANES), lambda b: (b, 0, 0)),
                pl.BlockSpec((1, nh), lambda b: (0, 0))]
    if past:
        args.append(past_logf)
        in_specs.append(pl.BlockSpec((1, past, nh), lambda b: (b, 0, 0)))
    return pl.pallas_call(
        functools.partial(_logf_kernel, t=t, past=past, nh=nh),
        grid=(bsz,),
        in_specs=in_specs,
        out_specs=[pl.BlockSpec((1, t, nh), lambda b: (b, 0, 0)),
                   pl.BlockSpec((1, past + t, nh), lambda b: (b, 0, 0))],
        out_shape=[jax.ShapeDtypeStruct((bsz, t, nh), F32),
                   jax.ShapeDtypeStruct((bsz, past + t, nh), F32)],
        compiler_params=_cparams(1),
        name="logf_cum",
    )(*args)


def _fox_kernel(q_ref, kt_ref, vt_ref, *rest, t, past, tq, tk):
    if past:
        kth_ref, vth_ref = rest[0], rest[1]
        rest = rest[2:]
    cq_ref, ck_ref, gm_ref, o_ref, kt_scr, vt_scr = rest
    hp = pl.program_id(1)
    qi = pl.program_id(2)
    nq = t // tq

    @pl.when(qi == 0)
    def _():
        if past:
            kt_scr[:, 0:past] = kth_ref[0, 0].astype(BF16)
            vt_scr[:, 0:past] = vth_ref[0, 0].astype(BF16)
        kt_scr[:, past:past + t] = kt_ref[0].astype(BF16)
        vt_scr[:, past:past + t] = vt_ref[0].astype(BF16)

    q = q_ref[0]
    lane = lax.broadcasted_iota(jnp.int32, q.shape, 1)
    cq_blk = cq_ref[0]
    head_lane = lax.broadcasted_iota(jnp.int32, cq_blk.shape, 1)
    r = lax.broadcasted_iota(jnp.int32, (tq, tq), 0)
    c = lax.broadcasted_iota(jnp.int32, (tq, tq), 1)
    causal = c <= r

    def head_row(blk, head):
        rid = lax.broadcasted_iota(jnp.int32, blk.shape, 0)
        return jnp.sum(jnp.where(rid == head, blk, 0.0), axis=0, keepdims=True)

    qh, cq = [], []
    for hh in range(2):
        in_half = (lane >= hh * HD_FOX) & (lane < (hh + 1) * HD_FOX)
        qh.append(jnp.where(in_half, q, jnp.zeros_like(q)))
        cq.append(jnp.sum(jnp.where(head_lane == 2 * hp + hh, cq_blk, 0.0), axis=-1, keepdims=True))

    def block(carry, off, width, mask):
        kt = kt_scr[:, pl.ds(off, width)]
        vt = vt_scr[:, pl.ds(off, width)]
        ck_all = ck_ref[0, :, pl.ds(off, width)]
        pv = lambda p: _dot_nt(p, vt)
        out = []
        for hh in range(2):
            s = _dot(qh[hh], kt) + cq[hh] - head_row(ck_all, 2 * hp + hh)
            if mask is not None:
                s = jnp.where(mask, s, NEG)
            out.append(_online_update(*carry[hh], s, pv))
        return tuple(out)

    carry = (_softmax_init(tq, LANES), _softmax_init(tq, LANES))
    if nq == 1:
        if past:
            carry = block(carry, 0, past, None)
        d0 = past
    else:
        carry = lax.fori_loop(
            0, qi, lambda j, cr: block(cr, pl.multiple_of(j * tk, tk), tk, None), carry)
        d0 = pl.multiple_of(qi * tq, tq)
    carry = block(carry, d0, tq, causal)

    (_, l0, a0), (_, l1, a1) = carry
    o = jnp.where(lane < HD_FOX, a0 / l0, a1 / l1)
    sq = o * o
    ms0 = jnp.sum(jnp.where(lane < HD_FOX, sq, 0.0), axis=-1, keepdims=True) / HD_FOX
    ms1 = jnp.sum(jnp.where(lane >= HD_FOX, sq, 0.0), axis=-1, keepdims=True) / HD_FOX
    inv = jnp.where(lane < HD_FOX, lax.rsqrt(ms0 + EPS), lax.rsqrt(ms1 + EPS))
    o_ref[0] = (o * inv * gm_ref[...]).astype(BF16)


def _fox_attn(q, kt, vt, kth, vth, li, cum, gm):
    bsz, t, hw = q.shape
    npair = hw // LANES
    past = 0 if kth is None else kth.shape[-1]
    nh = cum.shape[-1]
    tq, tk, nq = _attn_blocks(t, past)
    cq = cum[:, past:, :]
    ck = jnp.swapaxes(cum, 1, 2)
    kv_spec = pl.BlockSpec((1, LANES, t), lambda b, h, i: (b, h, 0))
    args = [q, kt, vt]
    in_specs = [pl.BlockSpec((1, tq, LANES), lambda b, h, i: (b, i, h)), kv_spec, kv_spec]
    if past:
        h_spec = pl.BlockSpec((1, 1, LANES, past), lambda b, h, i: (li, b, h, 0))
        args += [kth, vth]
        in_specs += [h_spec, h_spec]
    args += [cq, ck, gm]
    in_specs += [pl.BlockSpec((1, tq, nh), lambda b, h, i: (b, i, 0)),
                 pl.BlockSpec((1, nh, past + t), lambda b, h, i: (b, 0, 0)),
                 pl.BlockSpec((1, LANES), lambda b, h, i: (0, h))]
    return pl.pallas_call(
        functools.partial(_fox_kernel, t=t, past=past, tq=tq, tk=tk),
        grid=(bsz, npair, nq),
        in_specs=in_specs,
        out_specs=pl.BlockSpec((1, tq, LANES), lambda b, h, i: (b, i, h)),
        out_shape=jax.ShapeDtypeStruct((bsz, t, hw), BF16),
        scratch_shapes=[pltpu.VMEM((LANES, past + t), BF16),
                        pltpu.VMEM((LANES, past + t), BF16)],
        compiler_params=_cparams(3),
        name="fox_attn",
    )(*args)


def _oproj_kernel(x_ref, a_ref, b_ref, c_ref, w_ref, o_ref, *, d1, d2):
    acc = _dot(a_ref[...], w_ref[0:d1, :])
    acc += _dot(b_ref[...], w_ref[d1:d2, :])
    acc += _dot(c_ref[...], w_ref[d2:, :])
    o_ref[...] = x_ref[...] + acc


def _oproj(x, a, b, c, w):
    n, d = x.shape
    tm = min(ROW_BLOCK, n)
    d1 = a.shape[1]
    d2 = d1 + b.shape[1]
    row = lambda width: pl.BlockSpec((tm, width), lambda i: (i, 0))
    return pl.pallas_call(
        functools.partial(_oproj_kernel, d1=d1, d2=d2),
        grid=(n // tm,),
        in_specs=[row(d), row(a.shape[1]), row(b.shape[1]), row(c.shape[1]), _resident(w.shape)],
        out_specs=row(d),
        out_shape=jax.ShapeDtypeStruct((n, d), F32),
        compiler_params=_cparams(1),
        name="oproj",
    )(x, a, b, c, w)


def _ffn_kernel(x_ref, g_ref, wa_ref, wb_ref, cw_ref, cb_ref, wo_ref, st_ref, o_ref, tail_ref,
                xn_scr, acc_scr, halo_scr, *, tm, t, tf):
    i = pl.program_id(0)
    c = pl.program_id(1)

    @pl.when(c == 0)
    def _():
        xn_scr[...] = (_rms_rows(x_ref[...]) * g_ref[...]).astype(BF16)
        acc_scr[...] = jnp.zeros_like(acc_scr)

    xn = xn_scr[...]
    a = _dot(xn, wa_ref[...])
    b = _dot(xn, wb_ref[...])
    rows = lax.broadcasted_iota(jnp.int32, (tm, tf), 0)
    prev1 = pltpu.roll(a, 1, 0)
    prev2 = pltpu.roll(a, 2, 0)
    tail_rows = 8
    if t >= tm:
        @pl.when((i % (t // tm)) == 0)
        def _():
            halo_scr[c] = st_ref[0]

        src = halo_scr[c]
        h0 = src[tail_rows - 2:tail_rows - 1, :]
        h1 = src[tail_rows - 1:tail_rows, :]
        a1 = jnp.where(rows == 0, h1, prev1)
        a2 = jnp.where(rows == 0, h0, jnp.where(rows == 1, h1, prev2))
        last = a[tm - tail_rows:tm, :]
        halo_scr[c] = last
        tail_ref[0] = last
    else:
        nseq = tm // t
        st = st_ref[...]
        h0 = jnp.broadcast_to(st[:, tail_rows - 2:tail_rows - 1, :], (nseq, t, tf)).reshape(tm, tf)
        h1 = jnp.broadcast_to(st[:, tail_rows - 1:tail_rows, :], (nseq, t, tf)).reshape(tm, tf)
        rmod = rows & (t - 1)
        a1 = jnp.where(rmod == 0, h1, prev1)
        a2 = jnp.where(rmod == 0, h0, jnp.where(rmod == 1, h1, prev2))
        tail_ref[...] = a.reshape(nseq, t, tf)[:, t - tail_rows:, :]
    ac = cw_ref[2:3, :] * a + cw_ref[1:2, :] * a1 + cw_ref[0:1, :] * a2 + cb_ref[...]
    hh = ac * (1.0 / (1.0 + jnp.exp(-ac))) * b
    acc_scr[...] += _dot(hh.astype(BF16), wo_ref[...])

    @pl.when(c == pl.num_programs(1) - 1)
    def _():
        o_ref[...] = x_ref[...] + acc_scr[...]


def _ffn(x, t, g, wa, wb, cw, cb, wo, st):
    n, d = x.shape
    ffp = wa.shape[1]
    tf = min(FF_BLOCK, ffp)
    tm = min(ROW_BLOCK, n)
    assert ffp % tf == 0 and n % tm == 0 and (t % tm == 0 or tm % t == 0)
    assert t & (t - 1) == 0
    nseq = n // t
    if t >= tm:
        per = t // tm
        st_spec = pl.BlockSpec((1, 8, tf), lambda i, c: (i // per, 0, c))
        tail_spec = pl.BlockSpec((1, 8, tf), lambda i, c: (i, 0, c))
    else:
        per = 1
        st_spec = tail_spec = pl.BlockSpec((tm // t, 8, tf), lambda i, c: (i, 0, c))
    x_new, tail = pl.pallas_call(
        functools.partial(_ffn_kernel, tm=tm, t=t, tf=tf),
        grid=(n // tm, ffp // tf),
        in_specs=[pl.BlockSpec((tm, d), lambda i, c: (i, 0)),
                  pl.BlockSpec((1, d), lambda i, c: (0, 0)),
                  pl.BlockSpec((d, tf), lambda i, c: (0, c)),
                  pl.BlockSpec((d, tf), lambda i, c: (0, c)),
                  pl.BlockSpec((8, tf), lambda i, c: (0, c)),
                  pl.BlockSpec((1, tf), lambda i, c: (0, c)),
                  pl.BlockSpec((tf, d), lambda i, c: (c, 0)),
                  st_spec],
        out_specs=[pl.BlockSpec((tm, d), lambda i, c: (i, 0)), tail_spec],
        out_shape=[jax.ShapeDtypeStruct((n, d), F32),
                   jax.ShapeDtypeStruct((nseq * per, 8, ffp), F32)],
        scratch_shapes=[pltpu.VMEM((tm, d), BF16),
                        pltpu.VMEM((tm, d), F32),
                        pltpu.VMEM((ffp // tf, 8, tf), F32)],
        compiler_params=_cparams(2),
        name="ffn",
    )(x, g, wa, wb, cw, cb, wo, st)
    return x_new, tail.reshape(nseq, per, 8, ffp)[:, per - 1]


def _norm_kernel(x_ref, g_ref, o_ref):
    o_ref[...] = _rms_rows(x_ref[...]) * g_ref[...]


def _final_norm(x, g):
    n, d = x.shape
    tm = min(ROW_BLOCK, n)
    return pl.pallas_call(
        _norm_kernel,
        grid=(n // tm,),
        in_specs=[pl.BlockSpec((tm, d), lambda i: (i, 0)), pl.BlockSpec((1, d), lambda i: (0, 0))],
        out_specs=pl.BlockSpec((tm, d), lambda i: (i, 0)),
        out_shape=jax.ShapeDtypeStruct((n, d), F32),
        compiler_params=_cparams(1),
        name="final_norm",
    )(x, g)


def _prep_layer(p, dims):
    d_ssm, hd, hf, d_ff, ffp = dims
    wd, wf = hd * LANES, hf * HD_FOX
    o_qd, o_kd, o_vd = d_ssm, d_ssm + wd, d_ssm + 2 * wd
    o_qf, o_kf, o_vf, o_fg = d_ssm + 3 * wd, d_ssm + 3 * wd + wf, d_ssm + 3 * wd + 2 * wf, d_ssm + 3 * wd + 3 * wf
    w_in = p['w_in']
    w_fg = jnp.pad(w_in[:, o_fg:], ((0, 0), (0, LANES - hf)))
    q = dict(p)
    q['w_in_n'] = jnp.concatenate([w_in[:, :o_kd], w_in[:, o_qf:o_kf], w_fg, w_in[:, o_vd:o_qf]],
                                  axis=1).astype(BF16)
    q['w_in_t'] = jnp.concatenate([w_in[:, o_kd:o_vd], w_in[:, o_kf:o_fg]], axis=1).T.astype(BF16)
    q['w_glu'] = p['w_glu'].astype(BF16)
    q['w_o'] = p['w_o'].astype(BF16)
    padc = ((0, 0), (0, ffp - d_ff))
    q['w_a'] = jnp.pad(p['w_ffn_in'][:, :d_ff], padc).astype(BF16)
    q['w_b'] = jnp.pad(p['w_ffn_in'][:, d_ff:], padc).astype(BF16)
    q['w_ffn_out'] = jnp.pad(p['w_ffn_out'], ((0, ffp - d_ff), (0, 0))).astype(BF16)
    q['conv_w'] = jnp.pad(p['ffn_conv_w'], ((0, 8 - CONV_W), (0, ffp - d_ff)))
    q['conv_b'] = jnp.pad(p['ffn_conv_b'], (0, ffp - d_ff))[None, :]
    q['s5'] = _s5_params(p['ssm_lam_re'], p['ssm_lam_im'], p['ssm_log_dt'], p['ssm_b_re'],
                         p['ssm_b_im'], p['ssm_c_re'], p['ssm_c_im'], p['ssm_d'])
    q['lam'] = (jnp.exp(jnp.sum(p['diff_lam_q1'] * p['diff_lam_k1']))
                - jnp.exp(jnp.sum(p['diff_lam_q2'] * p['diff_lam_k2'])) + p['lam_init'])
    return q


def _layer(x, bsz, t, li, hist, st, p, dims, bias_tiles, lam_init):
    d_ssm, hd, hf, d_ff, ffp = dims
    kdt_h, vd_h, kft_h, vft_h = hist
    lf_c, sr_c, si_c, cv_c = st
    n, d = x.shape
    wd, wf = hd * LANES, hf * HD_FOX
    u, qd, qf, fg, vd, kdt, kft, vft = _inproj(x, t, p['g_norm_mix'][None, :], p['w_in_n'], p['w_in_t'],
                                               d_ssm, wd, wf)
    gm = p['g_mix_out'][None, :]
    y_ssm, sr, si = _s5(u.reshape(bsz, t, d_ssm), sr_c, si_c, p['s5'])
    o_ssm = _glu(y_ssm, p['w_glu'], p['b_glu'][None, :], gm[:, :d_ssm])

    vd = vd.reshape(bsz, t * hd, DV_DIFF)
    o_diff = _diff_attn(qd.reshape(bsz, t, wd), kdt, vd, kdt_h, vd_h, li, p['lam'], p['rel_bias'],
                        bias_tiles, gm[:, d_ssm:d_ssm + wd], lam_init)

    logf, cum = _logf_cum(fg.reshape(bsz, t, LANES), p['fox_b_f'], lf_c)
    o_fox = _fox_attn(qf.reshape(bsz, t, wf), kft, vft, kft_h, vft_h, li, cum, gm[:, d_ssm + wd:])

    x = _oproj(x, o_ssm, o_diff.reshape(n, wd), o_fox.reshape(n, wf), p['w_o'])

    x, tail = _ffn(x, t, p['g_norm_ffn'][None, :], p['w_a'], p['w_b'], p['conv_w'], p['conv_b'],
                   p['w_ffn_out'], cv_c)
    new_st = (kdt, vd, kft, vft, logf, sr, si, tail[:, 8 - (CONV_W - 1):, :d_ff])
    return x, new_st


def kernel(x_prompt, x_sample, cache_diff_k, cache_diff_v, cache_fox_k, cache_fox_v, cache_fox_logf, state_ssm_re, state_ssm_im, state_ffn_conv, g_norm_mix, w_in, ssm_lam_re, ssm_lam_im, ssm_log_dt, ssm_b_re, ssm_b_im, ssm_c_re, ssm_c_im, ssm_d, w_glu, b_glu, diff_lam_q1, diff_lam_k1, diff_lam_q2, diff_lam_k2, rel_bias, fox_b_f, g_mix_out, w_o, g_norm_ffn, w_ffn_in, ffn_conv_w, ffn_conv_b, w_ffn_out, g_final):
    per_layer = {'g_norm_mix': g_norm_mix, 'w_in': w_in, 'ssm_lam_re': ssm_lam_re,
                 'ssm_lam_im': ssm_lam_im, 'ssm_log_dt': ssm_log_dt, 'ssm_b_re': ssm_b_re,
                 'ssm_b_im': ssm_b_im, 'ssm_c_re': ssm_c_re, 'ssm_c_im': ssm_c_im, 'ssm_d': ssm_d,
                 'w_glu': w_glu, 'b_glu': b_glu, 'diff_lam_q1': diff_lam_q1, 'diff_lam_k1': diff_lam_k1,
                 'diff_lam_q2': diff_lam_q2, 'diff_lam_k2': diff_lam_k2, 'fox_b_f': fox_b_f,
                 'g_mix_out': g_mix_out, 'w_o': w_o, 'g_norm_ffn': g_norm_ffn, 'w_ffn_in': w_ffn_in,
                 'ffn_conv_w': ffn_conv_w, 'ffn_conv_b': ffn_conv_b, 'w_ffn_out': w_ffn_out}
    depth = w_in.shape[0]
    bp, tp, d_model = x_prompt.shape
    bs, ts, _ = x_sample.shape
    past = cache_diff_k.shape[2]
    g_ssm = ssm_lam_re.shape[1]
    d_ssm = g_ssm * SSM_GROUP
    hd = cache_diff_k.shape[3]
    hf = cache_fox_k.shape[3]
    wd, wf = hd * LANES, hf * HD_FOX
    d_ff = w_ffn_out.shape[1]
    ffp = -(-d_ff // FF_BLOCK) * FF_BLOCK if d_ff > FF_BLOCK else d_ff
    dims = (d_ssm, hd, hf, d_ff, ffp)

    def conv_state_block(s):
        return jnp.pad(s, ((0, 0), (8 - (CONV_W - 1), 0), (0, ffp - d_ff)))

    hist_sample = (jnp.transpose(cache_diff_k, (0, 1, 3, 4, 5, 2)).reshape(depth, bs, wd, past),
                   cache_diff_v.reshape(depth, bs, past * hd, DV_DIFF),
                   jnp.transpose(cache_fox_k, (0, 1, 3, 4, 2)).reshape(depth, bs, wf, past),
                   jnp.transpose(cache_fox_v, (0, 1, 3, 4, 2)).reshape(depth, bs, wf, past))
    hist_prompt = (None, None, None, None)
    st_prompt = (None, jnp.zeros((bp, g_ssm, P_SSM), F32), jnp.zeros((bp, g_ssm, P_SSM), F32),
                 conv_state_block(jnp.zeros((bp, CONV_W - 1, d_ff), F32)))
    bias_p = _bias_tiles(rel_bias, *_attn_blocks(tp, 0)[:2])
    bias_s = _bias_tiles(rel_bias, *_attn_blocks(ts, past)[:2])

    xp = x_prompt.reshape(bp * tp, d_model)
    xs = x_sample.reshape(bs * ts, d_model)
    outs_p, outs_s = [], []
    for li in range(depth):
        lam_init = 0.8 - 0.6 * math.exp(-0.3 * li)
        p = {name: arr[li] for name, arr in per_layer.items()}
        p['rel_bias'] = rel_bias
        p['lam_init'] = lam_init
        p = _prep_layer(p, dims)
        xp, sp = _layer(xp, bp, tp, li, hist_prompt, st_prompt, p, dims, bias_p, lam_init)
        st_sample = (cache_fox_logf[li], state_ssm_re[li], state_ssm_im[li],
                     conv_state_block(state_ffn_conv[li]))
        xs, ss = _layer(xs, bs, ts, li, hist_sample, st_sample, p, dims, bias_s, lam_init)
        outs_p.append(sp)
        outs_s.append(ss)

    def assemble(outs, bsz, t):
        kdt, vd, kft, vft, logf, sr, si, cv = [jnp.stack([s[i] for s in outs]) for i in range(8)]
        return (jnp.transpose(kdt.reshape(depth, bsz, hd, 2, DQK_DIFF, t), (0, 1, 5, 2, 3, 4)),
                vd.reshape(depth, bsz, t, hd, DV_DIFF),
                jnp.transpose(kft.reshape(depth, bsz, hf, HD_FOX, t), (0, 1, 4, 2, 3)),
                jnp.transpose(vft.reshape(depth, bsz, hf, HD_FOX, t), (0, 1, 4, 2, 3)),
                logf, sr, si, cv)

    y_prompt = _final_norm(xp, g_final[None, :]).reshape(bp, tp, d_model)
    y_sample = _final_norm(xs, g_final[None, :]).reshape(bs, ts, d_model)
    return (y_prompt, y_sample, *assemble(outs_p, bp, tp), *assemble(outs_s, bs, ts))
```

```python
import functools
import math

import numpy as np
import jax
import jax.numpy as jnp
from jax import lax
from jax.experimental import pallas as pl
from jax.experimental.pallas import tpu as pltpu

F32 = jnp.float32
BF16 = jnp.bfloat16

EPS = 1e-6
CHUNK = 64
S5_BLOCK = 128
SSM_GROUP = 16
P_SSM = 64
DQK_DIFF = 64
DV_DIFF = 128
HD_FOX = 64
N_BUCKETS = 32
MAX_DISTANCE = 128
CONV_W = 3
LANES = 128
ATTN_BLOCK = 512
FF_BLOCK = 512
FF_SUB = 256
ROW_BLOCK = 512
V7X_VMEM_LIMIT = 56 * 1024 * 1024
NEG = -1e30
LOG2E = math.log2(math.e)


def _cparams(n_axes):
    return pltpu.CompilerParams(dimension_semantics=("arbitrary",) * n_axes,
                                vmem_limit_bytes=V7X_VMEM_LIMIT)


def _resident(shape):
    nd = len(shape)
    return pl.BlockSpec(shape, lambda *_: (0,) * nd, pipeline_mode=pl.Buffered(1))


def _rms_rows(x):
    return x * lax.rsqrt(jnp.mean(x * x, axis=-1, keepdims=True) + EPS)


def _dot(a, b):
    return jnp.dot(a, b, preferred_element_type=F32)


def _dot_nt(a, b):
    return lax.dot_general(a, b, (((1,), (1,)), ((), ())), preferred_element_type=F32)


def _inproj_kernel(x_ref, g_ref, wn_ref, wt_ref, u_ref, qd_ref, qf_ref, fg_ref, vd_ref,
                   kdt_ref, kft_ref, vft_ref, *, d_ssm, wd, wf, ns, u_slabs):
    tm = x_ref.shape[0]
    xn = (_rms_rows(x_ref[...]) * g_ref[...]).astype(BF16)
    c = 0
    qd_ref[...] = (_dot(xn, wn_ref[:, c:c + wd]) * (DQK_DIFF ** -0.5 * LOG2E)).astype(BF16)
    c += wd
    qf_ref[...] = (_dot(xn, wn_ref[:, c:c + wf]) * (HD_FOX ** -0.5 * LOG2E)).astype(BF16)
    c += wf
    fg_ref[...] = _dot(xn, wn_ref[:, c:c + LANES])
    c += LANES
    vd = _dot(xn, wn_ref[:, c:c + wd])
    nh = wd // DV_DIFF
    for h in range(nh):
        vd_ref[pl.ds(h, tm, stride=nh), :] = vd[:, h * DV_DIFF:(h + 1) * DV_DIFF]
    ts = tm // ns
    zt = _dot_nt(wt_ref[0:d_ssm, :], xn)
    if u_slabs:
        for k in range(u_slabs):
            u_ref[k, 0] = zt[:, k * S5_BLOCK:(k + 1) * S5_BLOCK]
    else:
        for s in range(ns):
            u_ref[s] = zt[:, s * ts:(s + 1) * ts]
    r = d_ssm
    for o_ref, width in ((kdt_ref, wd), (kft_ref, wf), (vft_ref, wf)):
        zt = _dot_nt(wt_ref[r:r + width, :], xn)
        for s in range(ns):
            o_ref[s] = zt[:, s * ts:(s + 1) * ts]
        r += width


def _inproj(x, t, g, wn, wt, d_ssm, wd, wf):
    n, d = x.shape
    tm = min(ROW_BLOCK, n)
    nseq = n // t
    nh = wd // DV_DIFF
    assert n % tm == 0 and (t % tm == 0 or tm % t == 0)
    if t >= tm:
        per, ns = t // tm, 1
        tspec = lambda width: pl.BlockSpec((1, width, tm), lambda i: (i // per, 0, i % per))
        u_slabs = tm // S5_BLOCK
        u_spec = pl.BlockSpec((u_slabs, 1, d_ssm, S5_BLOCK), lambda i: (i % per, i // per, 0, 0))
        u_shape = (t // S5_BLOCK, nseq, d_ssm, S5_BLOCK)
    else:
        ns = tm // t
        tspec = lambda width: pl.BlockSpec((ns, width, t), lambda i: (i, 0, 0))
        u_slabs = 0
        u_spec = tspec(d_ssm)
        u_shape = (nseq, d_ssm, t)
    row = lambda width: pl.BlockSpec((tm, width), lambda i: (i, 0))
    return pl.pallas_call(
        functools.partial(_inproj_kernel, d_ssm=d_ssm, wd=wd, wf=wf, ns=ns, u_slabs=u_slabs),
        grid=(n // tm,),
        in_specs=[row(d), _resident((1, d)), _resident(wn.shape), _resident(wt.shape)],
        out_specs=[u_spec, row(wd), row(wf), row(LANES),
                   pl.BlockSpec((tm * nh, DV_DIFF), lambda i: (i, 0)),
                   tspec(wd), tspec(wf), tspec(wf)],
        out_shape=[jax.ShapeDtypeStruct(u_shape, F32),
                   jax.ShapeDtypeStruct((n, wd), BF16),
                   jax.ShapeDtypeStruct((n, wf), BF16),
                   jax.ShapeDtypeStruct((n, LANES), F32),
                   jax.ShapeDtypeStruct((n * nh, DV_DIFF), F32),
                   jax.ShapeDtypeStruct((nseq, wd, t), F32),
                   jax.ShapeDtypeStruct((nseq, wf, t), F32),
                   jax.ShapeDtypeStruct((nseq, wf, t), F32)],
        compiler_params=_cparams(1),
        name="inproj",
    )(x, g, wn, wt)


def _s5_block_ops(u, kt_ref, p_ref, q_ref, a_ref, d_ref, h0_ref, ht_ref, m_scr, v_scr, hin_scr,
                  nb, bsz, blk, emit):
    width = SSM_GROUP * blk
    lane = lax.broadcasted_iota(jnp.int32, (blk, width), 1)
    row = lax.broadcasted_iota(jnp.int32, (blk, width), 0)
    causal = (lane & (blk - 1)) >= row
    for h in range(SSM_GROUP):
        base = jnp.broadcast_to(kt_ref[0, h:h + 1, :], (blk, width))
        shifted = pltpu.roll(base, 0, 1, stride=1, stride_axis=0)
        m_scr[h * blk:(h + 1) * blk, :] = jnp.where(causal, shifted, 0.0).astype(BF16)

    ub = u.astype(BF16)
    emit(_dot(ub, m_scr[...]) + d_ref[0] * u, None)
    v_scr[...] = _dot(ub, p_ref[0])

    a_same = a_ref[0, 0:1, :]
    a_cross = a_ref[0, 1:2, :]
    h = h0_ref[0]
    for c in range(nb):
        hin_scr[c * bsz:(c + 1) * bsz, :] = h
        h = a_same * h + a_cross * pltpu.roll(h, P_SSM, 1) + v_scr[c * bsz:(c + 1) * bsz, :]
    ht_ref[0] = h
    emit(None, _dot(hin_scr[...].astype(BF16), q_ref[0]))


def _s5_kernel(u_ref, kt_ref, p_ref, q_ref, a_ref, d_ref, h0_ref, y_ref, ht_ref,
               m_scr, v_scr, hin_scr, *, nb, bsz, blk):
    def emit(y_local, y_state):
        if y_local is not None:
            y_ref[0] = y_local
        else:
            y_ref[0] += y_state

    _s5_block_ops(u_ref[0], kt_ref, p_ref, q_ref, a_ref, d_ref, h0_ref, ht_ref, m_scr, v_scr, hin_scr,
                  nb, bsz, blk, emit)


def _s5_seq_kernel(u_ref, kt_ref, p_ref, q_ref, a_ref, d_ref, h0_ref, y_ref, ht_ref,
                   m_scr, v_scr, hin_scr, u_scr, y_scr, *, nb, bsz, blk):
    rows = nb * bsz
    for h in range(SSM_GROUP):
        u_scr[:, h * blk:(h + 1) * blk] = u_ref[:, :, h, :].reshape(rows, blk)

    def emit(y_local, y_state):
        if y_local is not None:
            y_scr[...] = y_local
        else:
            y_scr[...] += y_state

    _s5_block_ops(u_scr[...], kt_ref, p_ref, q_ref, a_ref, d_ref, h0_ref, ht_ref, m_scr, v_scr, hin_scr,
                  nb, bsz, blk, emit)
    for h in range(SSM_GROUP):
        y_ref[:, :, h, :] = y_scr[:, h * blk:(h + 1) * blk].reshape(nb, bsz, blk)


def _s5_params(lam_re, lam_im, log_dt, b_re, b_im, c_re, c_im, d_skip, blk):
    hp = lax.Precision.HIGHEST
    g = lam_re.shape[0]
    dt = jnp.exp(log_dt)[:, None]
    k = jnp.arange(0, blk + 1, dtype=F32)[:, None, None]
    mag = jnp.exp(lam_re * dt * k)
    pr = mag * jnp.cos(lam_im * dt * k)
    pi = mag * jnp.sin(lam_im * dt * k)
    ar, ai = pr[1], pi[1]
    den = lam_re * lam_re + lam_im * lam_im
    fr = ((ar - 1.0) * lam_re + ai * lam_im) / den
    fi = (ai * lam_re - (ar - 1.0) * lam_im) / den
    bbr = fr[..., None] * b_re - fi[..., None] * b_im
    bbi = fr[..., None] * b_im + fi[..., None] * b_re
    car = c_re[None] * pr[:, :, None, :] - c_im[None] * pi[:, :, None, :]
    cai = c_re[None] * pi[:, :, None, :] + c_im[None] * pr[:, :, None, :]
    kt = (jnp.einsum('kgap,gph->ghak', car[:blk], bbr, precision=hp)
          - jnp.einsum('kgap,gph->ghak', cai[:blk], bbi, precision=hp))
    kt = kt.reshape(g, SSM_GROUP, SSM_GROUP * blk)
    prr = pr[blk - 1::-1][:blk]
    pir = pi[blk - 1::-1][:blk]
    p_re = prr[:, :, :, None] * bbr[None] - pir[:, :, :, None] * bbi[None]
    p_im = prr[:, :, :, None] * bbi[None] + pir[:, :, :, None] * bbr[None]
    pmat = jnp.concatenate([jnp.transpose(p_re, (1, 3, 0, 2)), jnp.transpose(p_im, (1, 3, 0, 2))],
                           axis=-1).reshape(g, SSM_GROUP * blk, 2 * P_SSM)
    q_re = jnp.transpose(car[1:], (1, 3, 2, 0))
    q_im = -jnp.transpose(cai[1:], (1, 3, 2, 0))
    qmat = jnp.concatenate([q_re, q_im], axis=1).reshape(g, 2 * P_SSM, SSM_GROUP * blk)
    a_blk = jnp.stack([jnp.concatenate([pr[blk], pr[blk]], -1),
                       jnp.concatenate([-pi[blk], pi[blk]], -1)], axis=1)
    dvec = jnp.repeat(d_skip, blk, axis=-1)[:, None, :]
    return kt, pmat.astype(BF16), qmat.astype(BF16), a_blk, dvec


def _s5_call(kern, u_arr, u_spec, sp, h_re, h_im, nb, bsz, blk, extra_scratch):
    kt, pmat, qmat, a_blk, dvec = sp
    g = kt.shape[0]
    width = SSM_GROUP * blk
    rows = nb * bsz
    h0 = jnp.transpose(jnp.concatenate([h_re, h_im], axis=-1), (1, 0, 2))
    y, ht = pl.pallas_call(
        functools.partial(kern, nb=nb, bsz=bsz, blk=blk),
        grid=(g,),
        in_specs=[u_spec,
                  pl.BlockSpec((1, SSM_GROUP, width), lambda i: (i, 0, 0)),
                  pl.BlockSpec((1, width, 2 * P_SSM), lambda i: (i, 0, 0)),
                  pl.BlockSpec((1, 2 * P_SSM, width), lambda i: (i, 0, 0)),
                  pl.BlockSpec((1, 2, 2 * P_SSM), lambda i: (i, 0, 0)),
                  pl.BlockSpec((1, 1, width), lambda i: (i, 0, 0)),
                  pl.BlockSpec((1, bsz, 2 * P_SSM), lambda i: (i, 0, 0))],
        out_specs=[u_spec, pl.BlockSpec((1, bsz, 2 * P_SSM), lambda i: (i, 0, 0))],
        out_shape=[jax.ShapeDtypeStruct(u_arr.shape, F32),
                   jax.ShapeDtypeStruct((g, bsz, 2 * P_SSM), F32)],
        scratch_shapes=[pltpu.VMEM((width, width), BF16),
                        pltpu.VMEM((rows, 2 * P_SSM), F32),
                        pltpu.VMEM((rows, 2 * P_SSM), F32)] + extra_scratch,
        compiler_params=_cparams(1),
        name="s5",
    )(u_arr, kt, pmat, qmat, a_blk, dvec, h0)
    ht = jnp.transpose(ht, (1, 0, 2))
    return y, ht[..., :P_SSM], ht[..., P_SSM:]


def _s5_short(ut, h_re, h_im, sp):
    bsz, dssm, t = ut.shape
    g = dssm // SSM_GROUP
    nb = t // CHUNK
    width = SSM_GROUP * CHUNK
    rows = nb * bsz
    u = jnp.transpose(ut.reshape(bsz, g, SSM_GROUP, nb, CHUNK), (1, 3, 0, 2, 4)).reshape(g, rows, width)
    y, sr, si = _s5_call(_s5_kernel, u, pl.BlockSpec((1, rows, width), lambda i: (i, 0, 0)),
                         sp, h_re, h_im, nb, bsz, CHUNK, [])
    y = jnp.transpose(y.reshape(g, nb, bsz, SSM_GROUP, CHUNK), (2, 1, 4, 0, 3)).reshape(bsz * t, dssm)
    return y, sr, si


def _s5_seq(u4, h_re, h_im, sp):
    nb, bsz, dssm, blk = u4.shape
    rows, width = nb * bsz, SSM_GROUP * blk
    spec = pl.BlockSpec((nb, bsz, SSM_GROUP, blk), lambda i: (0, 0, i, 0))
    return _s5_call(_s5_seq_kernel, u4, spec, sp, h_re, h_im, nb, bsz, blk,
                    [pltpu.VMEM((rows, width), F32), pltpu.VMEM((rows, width), F32)])


def _glu_math(y, w_ref, b_ref, gm_ref):
    g = 0.5 * y * (1.0 + jnp.tanh(math.sqrt(2.0 / math.pi) * (y + 0.044715 * (y * y * y))))
    z = _dot(g.astype(BF16), w_ref[...]) + b_ref[...]
    yy = g * (1.0 / (1.0 + jnp.exp(-z)))
    return (_rms_rows(yy) * gm_ref[...]).astype(BF16)


def _glu_kernel(y_ref, w_ref, b_ref, gm_ref, o_ref):
    o_ref[...] = _glu_math(y_ref[...], w_ref, b_ref, gm_ref)


def _glu_seq_kernel(y_ref, w_ref, b_ref, gm_ref, o_ref, y_scr):
    blk = y_ref.shape[-1]
    for k in range(y_ref.shape[0]):
        y_scr[k * blk:(k + 1) * blk, :] = jnp.transpose(y_ref[k, 0])
    o_ref[...] = _glu_math(y_scr[...], w_ref, b_ref, gm_ref)


def _glu(y, w, b, gm):
    if y.ndim == 2:
        n, d = y.shape
        tm = min(ROW_BLOCK, n)
        kern, y_spec, scratch = _glu_kernel, pl.BlockSpec((tm, d), lambda i: (i, 0)), []
    else:
        nb, bsz, d, blk = y.shape
        n, tm = nb * bsz * blk, ROW_BLOCK
        per = nb * blk // tm
        kern = _glu_seq_kernel
        y_spec = pl.BlockSpec((tm // blk, 1, d, blk), lambda i: (i % per, i // per, 0, 0))
        scratch = [pltpu.VMEM((tm, d), F32)]
    return pl.pallas_call(
        kern,
        grid=(n // tm,),
        in_specs=[y_spec, _resident(w.shape), _resident((1, d)), _resident((1, d))],
        out_specs=pl.BlockSpec((tm, d), lambda i: (i, 0)),
        out_shape=jax.ShapeDtypeStruct((n, d), BF16),
        scratch_shapes=scratch,
        compiler_params=_cparams(1),
        name="glu",
    )(y, w, b, gm)


def _t5_bucket_np(rel):
    nb = N_BUCKETS // 2
    max_exact = nb // 2
    n = np.abs(rel)
    large = max_exact + (np.log(np.maximum(n, 1).astype(np.float32) / np.float32(max_exact))
                         / np.float32(math.log(MAX_DISTANCE / max_exact))
                         * np.float32(nb - max_exact)).astype(np.int32)
    large = np.minimum(large, nb - 1)
    return (np.where(rel > 0, nb, 0) + np.where(n < max_exact, n, large)).astype(np.int32)


def _bucket_thresholds():
    dist = np.arange(0, 4 * MAX_DISTANCE, dtype=np.int32)
    bk = _t5_bucket_np(-dist)
    assert np.all(np.diff(bk) >= 0) and bk[-1] == N_BUCKETS // 2 - 1
    assert np.array_equal(_t5_bucket_np(dist[1:]), bk[1:] + N_BUCKETS // 2)
    return [int(np.argmax(bk >= k)) for k in range(N_BUCKETS // 2)]


def _attn_blocks(t, past):
    tq = min(ATTN_BLOCK, t)
    nq = t // tq
    assert t % tq == 0 and tq % CHUNK == 0
    if nq == 1:
        tk = past if past else tq
    else:
        assert past == 0
        tk = tq
    return tq, tk, nq


def _online_update(m, l, acc, s, shift, pv):
    m_new = jnp.maximum(m, jnp.max(s, axis=-1, keepdims=True) + shift)
    alpha = jnp.exp2(m - m_new)
    p = jnp.exp2(s - (m_new - shift))
    l = alpha * l + jnp.sum(p, axis=-1, keepdims=True)
    acc = alpha * acc + pv(p.astype(BF16))
    return m_new, l, acc


def _softmax_init(tq, width):
    return (jnp.full((tq, 1), NEG, F32), jnp.zeros((tq, 1), F32), jnp.zeros((tq, width), F32))


def _bias_kernel(rb_ref, near_ref, diag_ref, *, nh, thr):
    h = pl.program_id(0)
    half = N_BUCKETS // 2

    def tile(shape, col_off):
        r = lax.broadcasted_iota(jnp.int32, shape, 0)
        c = lax.broadcasted_iota(jnp.int32, shape, 1)
        rel = c + col_off - r
        dist = jnp.abs(rel)
        back = jnp.full(shape, rb_ref[h], F32)
        fwd = jnp.full(shape, rb_ref[half * nh + h], F32)
        for k in range(1, half):
            far = dist >= thr[k]
            back = jnp.where(far, rb_ref[k * nh + h], back)
            fwd = jnp.where(far, rb_ref[(half + k) * nh + h], fwd)
        return jnp.where(rel > 0, fwd, back) * LOG2E, r, c

    near, _, _ = tile(near_ref.shape[1:], -near_ref.shape[2])
    near_ref[0] = near
    diag, r, c = tile(diag_ref.shape[1:], 0)
    diag_ref[0] = jnp.where((c // CHUNK) <= (r // CHUNK), diag, NEG)


def _bias_tiles(rel_bias, tq, tk):
    nh = rel_bias.shape[1]
    return pl.pallas_call(
        functools.partial(_bias_kernel, nh=nh, thr=_bucket_thresholds()),
        grid=(nh,),
        in_specs=[pl.BlockSpec(memory_space=pltpu.SMEM)],
        out_specs=[pl.BlockSpec((1, tq, tk), lambda h: (h, 0, 0)),
                   pl.BlockSpec((1, tq, tq), lambda h: (h, 0, 0))],
        out_shape=[jax.ShapeDtypeStruct((nh, tq, tk), F32),
                   jax.ShapeDtypeStruct((nh, tq, tq), F32)],
        compiler_params=_cparams(1),
        name="rel_bias_tiles",
    )(rel_bias.reshape(-1))


def _diff_kernel(lam_ref, rb_ref, q_ref, kt_ref, v_ref, *rest, t, past, tq, tk, nh, out_scale):
    if past:
        kth_ref, vh_ref = rest[0], rest[1]
        rest = rest[2:]
    bn_ref, bd_ref, gm_ref, o_ref, kt_scr, v_scr = rest
    h = pl.program_id(1)
    qi = pl.program_id(2)
    nq = t // tq

    @pl.when(qi == 0)
    def _():
        if past:
            kt_scr[:, 0:past] = kth_ref[0, 0].astype(BF16)
            v_scr[0:past, :] = vh_ref[0, 0, pl.ds(h, past, stride=nh), :].astype(BF16)
        kt_scr[:, past:past + t] = kt_ref[0].astype(BF16)
        v_scr[past:past + t, :] = v_ref[0, pl.ds(h, t, stride=nh), :].astype(BF16)

    q = q_ref[0]
    lane = lax.broadcasted_iota(jnp.int32, q.shape, 1)
    q1 = jnp.where(lane < DQK_DIFF, q, jnp.zeros_like(q))
    q2 = jnp.where(lane >= DQK_DIFF, q, jnp.zeros_like(q))
    far_bias = rb_ref[(N_BUCKETS // 2 - 1) * nh + h] * LOG2E

    def block(carry, off, width, bias_ref):
        kt = kt_scr[:, pl.ds(off, width)]
        vb = v_scr[pl.ds(off, width), :]
        c1, c2 = carry
        pv = lambda p: _dot(p, vb)
        s1, s2 = _dot(q1, kt), _dot(q2, kt)
        if bias_ref is None:
            shift = far_bias
        else:
            shift = 0.0
            s1, s2 = s1 + bias_ref[0], s2 + bias_ref[0]
        return _online_update(*c1, s1, shift, pv), _online_update(*c2, s2, shift, pv)

    carry = (_softmax_init(tq, DV_DIFF), _softmax_init(tq, DV_DIFF))
    if nq == 1:
        if past:
            carry = block(carry, 0, past, bn_ref)
        d0 = past
    else:
        carry = lax.fori_loop(
            0, qi - 1, lambda j, cr: block(cr, pl.multiple_of(j * tk, tk), tk, None), carry)
        carry = lax.cond(qi >= 1,
                         lambda cr: block(cr, pl.multiple_of((qi - 1) * tk, tk), tk, bn_ref),
                         lambda cr: cr, carry)
        d0 = pl.multiple_of(qi * tq, tq)
    carry = block(carry, d0, tq, bd_ref)

    (_, l1, a1), (_, l2, a2) = carry
    o = a1 / l1 - lam_ref[0] * (a2 / l2)
    o_ref[0] = (_rms_rows(o) * out_scale * gm_ref[...]).astype(BF16)


def _diff_attn(q, kt, v, kth, vh, li, lam, rel_bias, bias_tiles, gm, lam_init):
    bsz, t, hw = q.shape
    nh = hw // LANES
    past = 0 if kth is None else kth.shape[-1]
    tq, tk, nq = _attn_blocks(t, past)
    bias_near, bias_diag = bias_tiles
    smem = pl.BlockSpec(memory_space=pltpu.SMEM)
    args = [lam.reshape(1), rel_bias.reshape(-1), q, kt, v]
    in_specs = [smem, smem,
                pl.BlockSpec((1, tq, LANES), lambda b, h, i: (b, i, h)),
                pl.BlockSpec((1, LANES, t), lambda b, h, i: (b, h, 0)),
                pl.BlockSpec((1, t * nh, LANES), lambda b, h, i: (b, 0, 0))]
    if past:
        args += [kth, vh]
        in_specs += [pl.BlockSpec((1, 1, LANES, past), lambda b, h, i: (li, b, h, 0)),
                     pl.BlockSpec((1, 1, past * nh, LANES), lambda b, h, i: (li, b, 0, 0))]
    args += [bias_near, bias_diag, gm]
    in_specs += [pl.BlockSpec((1, tq, tk), lambda b, h, i: (h, 0, 0)),
                 pl.BlockSpec((1, tq, tq), lambda b, h, i: (h, 0, 0)),
                 pl.BlockSpec((1, LANES), lambda b, h, i: (0, h))]
    return pl.pallas_call(
        functools.partial(_diff_kernel, t=t, past=past, tq=tq, tk=tk, nh=nh, out_scale=1.0 - lam_init),
        grid=(bsz, nh, nq),
        in_specs=in_specs,
        out_specs=pl.BlockSpec((1, tq, LANES), lambda b, h, i: (b, i, h)),
        out_shape=jax.ShapeDtypeStruct((bsz, t, hw), BF16),
        scratch_shapes=[pltpu.VMEM((LANES, past + t), BF16),
                        pltpu.VMEM((past + t, LANES), BF16)],
        compiler_params=_cparams(3),
        name="diff_attn",
    )(*args)


def _logf_kernel(fg_ref, bf_ref, *rest, t, past, nh):
    if past:
        past_ref, logf_ref, cum_ref = rest
    else:
        logf_ref, cum_ref = rest
    x = fg_ref[0][:, :nh] + bf_ref[...]
    lf = -(jnp.maximum(-x, 0.0) + jnp.log1p(jnp.exp(-jnp.abs(x))))
    logf_ref[0] = lf

    def tri(n):
        r = lax.broadcasted_iota(jnp.int32, (n, n), 0)
        c = lax.broadcasted_iota(jnp.int32, (n, n), 1)
        return (r >= c).astype(F32)

    carry = jnp.zeros((1, nh), F32)
    pos = 0
    for src, length in ((None, past), (lf, t)):
        for r0 in range(0, length, LANES):
            n = min(LANES, length - r0)
            blk = past_ref[0, r0:r0 + n, :] if src is None else src[r0:r0 + n, :]
            cs = jnp.dot(tri(n), blk, preferred_element_type=F32,
                         precision=lax.Precision.HIGHEST) + carry
            cum_ref[0, pos:pos + n, :] = cs
            carry = cs[n - 1:n, :]
            pos += n


def _logf_cum(fg, b_f, past_logf):
    bsz, t, _ = fg.shape
    nh = b_f.shape[-1]
    past = 0 if past_logf is None else past_logf.shape[1]
    args = [fg, b_f.reshape(1, nh)]
    in_specs = [pl.BlockSpec((1, t, LANES), lambda b: (b, 0, 0)),
                pl.BlockSpec((1, nh), lambda b: (0, 0))]
    if past:
        args.append(past_logf)
        in_specs.append(pl.BlockSpec((1, past, nh), lambda b: (b, 0, 0)))
    return pl.pallas_call(
        functools.partial(_logf_kernel, t=t, past=past, nh=nh),
        grid=(bsz,),
        in_specs=in_specs,
        out_specs=[pl.BlockSpec((1, t, nh), lambda b: (b, 0, 0)),
                   pl.BlockSpec((1, past + t, nh), lambda b: (b, 0, 0))],
        out_shape=[jax.ShapeDtypeStruct((bsz, t, nh), F32),
                   jax.ShapeDtypeStruct((bsz, past + t, nh), F32)],
        compiler_params=_cparams(1),
        name="logf_cum",
    )(*args)


def _fox_kernel(q_ref, kt_ref, vt_ref, *rest, t, past, tq, tk):
    if past:
        kth_ref, vth_ref = rest[0], rest[1]
        rest = rest[2:]
    cq_ref, ck_ref, gm_ref, o_ref, kt_scr, vt_scr = rest
    hp = pl.program_id(1)
    qi = pl.program_id(2)
    nq = t // tq

    @pl.when(qi == 0)
    def _():
        if past:
            kt_scr[:, 0:past] = kth_ref[0, 0].astype(BF16)
            vt_scr[:, 0:past] = vth_ref[0, 0].astype(BF16)
        kt_scr[:, past:past + t] = kt_ref[0].astype(BF16)
        vt_scr[:, past:past + t] = vt_ref[0].astype(BF16)

    q = q_ref[0]
    lane = lax.broadcasted_iota(jnp.int32, q.shape, 1)
    cq_blk = cq_ref[0]
    head_lane = lax.broadcasted_iota(jnp.int32, cq_blk.shape, 1)
    r = lax.broadcasted_iota(jnp.int32, (tq, tq), 0)
    c = lax.broadcasted_iota(jnp.int32, (tq, tq), 1)
    causal = c <= r

    def head_row(blk, head):
        rid = lax.broadcasted_iota(jnp.int32, blk.shape, 0)
        return jnp.sum(jnp.where(rid == head, blk, 0.0), axis=0, keepdims=True)

    qh, cq = [], []
    for hh in range(2):
        in_half = (lane >= hh * HD_FOX) & (lane < (hh + 1) * HD_FOX)
        qh.append(jnp.where(in_half, q, jnp.zeros_like(q)))
        cq.append(jnp.sum(jnp.where(head_lane == 2 * hp + hh, cq_blk, 0.0), axis=-1, keepdims=True) * LOG2E)

    def block(carry, off, width, mask):
        kt = kt_scr[:, pl.ds(off, width)]
        vt = vt_scr[:, pl.ds(off, width)]
        ck_all = ck_ref[0, :, pl.ds(off, width)] * LOG2E
        pv = lambda p: _dot_nt(p, vt)
        out = []
        for hh in range(2):
            s = _dot(qh[hh], kt) - head_row(ck_all, 2 * hp + hh)
            if mask is not None:
                s = jnp.where(mask, s, NEG)
            out.append(_online_update(*carry[hh], s, cq[hh], pv))
        return tuple(out)

    carry = (_softmax_init(tq, LANES), _softmax_init(tq, LANES))
    if nq == 1:
        if past:
            carry = block(carry, 0, past, None)
        d0 = past
    else:
        carry = lax.fori_loop(
            0, qi, lambda j, cr: block(cr, pl.multiple_of(j * tk, tk), tk, None), carry)
        d0 = pl.multiple_of(qi * tq, tq)
    carry = block(carry, d0, tq, causal)

    (_, l0, a0), (_, l1, a1) = carry
    o = jnp.where(lane < HD_FOX, a0 / l0, a1 / l1)
    sq = o * o
    ms0 = jnp.sum(jnp.where(lane < HD_FOX, sq, 0.0), axis=-1, keepdims=True) / HD_FOX
    ms1 = jnp.sum(jnp.where(lane >= HD_FOX, sq, 0.0), axis=-1, keepdims=True) / HD_FOX
    inv = jnp.where(lane < HD_FOX, lax.rsqrt(ms0 + EPS), lax.rsqrt(ms1 + EPS))
    o_ref[0] = (o * inv * gm_ref[...]).astype(BF16)


def _fox_attn(q, kt, vt, kth, vth, li, cum, gm):
    bsz, t, hw = q.shape
    npair = hw // LANES
    past = 0 if kth is None else kth.shape[-1]
    nh = cum.shape[-1]
    tq, tk, nq = _attn_blocks(t, past)
    cq = cum[:, past:, :]
    ck = jnp.swapaxes(cum, 1, 2)
    kv_spec = pl.BlockSpec((1, LANES, t), lambda b, h, i: (b, h, 0))
    args = [q, kt, vt]
    in_specs = [pl.BlockSpec((1, tq, LANES), lambda b, h, i: (b, i, h)), kv_spec, kv_spec]
    if past:
        h_spec = pl.BlockSpec((1, 1, LANES, past), lambda b, h, i: (li, b, h, 0))
        args += [kth, vth]
        in_specs += [h_spec, h_spec]
    args += [cq, ck, gm]
    in_specs += [pl.BlockSpec((1, tq, nh), lambda b, h, i: (b, i, 0)),
                 pl.BlockSpec((1, nh, past + t), lambda b, h, i: (b, 0, 0)),
                 pl.BlockSpec((1, LANES), lambda b, h, i: (0, h))]
    return pl.pallas_call(
        functools.partial(_fox_kernel, t=t, past=past, tq=tq, tk=tk),
        grid=(bsz, npair, nq),
        in_specs=in_specs,
        out_specs=pl.BlockSpec((1, tq, LANES), lambda b, h, i: (b, i, h)),
        out_shape=jax.ShapeDtypeStruct((bsz, t, hw), BF16),
        scratch_shapes=[pltpu.VMEM((LANES, past + t), BF16),
                        pltpu.VMEM((LANES, past + t), BF16)],
        compiler_params=_cparams(3),
        name="fox_attn",
    )(*args)


def _oproj_kernel(x_ref, a_ref, b_ref, c_ref, w_ref, o_ref, *, d1, d2):
    acc = _dot(a_ref[...], w_ref[0:d1, :])
    acc += _dot(b_ref[...], w_ref[d1:d2, :])
    acc += _dot(c_ref[...], w_ref[d2:, :])
    o_ref[...] = x_ref[...] + acc


def _oproj(x, a, b, c, w):
    n, d = x.shape
    tm = min(ROW_BLOCK, n)
    d1 = a.shape[1]
    d2 = d1 + b.shape[1]
    row = lambda width: pl.BlockSpec((tm, width), lambda i: (i, 0))
    return pl.pallas_call(
        functools.partial(_oproj_kernel, d1=d1, d2=d2),
        grid=(n // tm,),
        in_specs=[row(d), row(a.shape[1]), row(b.shape[1]), row(c.shape[1]), _resident(w.shape)],
        out_specs=row(d),
        out_shape=jax.ShapeDtypeStruct((n, d), F32),
        compiler_params=_cparams(1),
        name="oproj",
    )(x, a, b, c, w)


def _ffn_kernel(x_ref, g_ref, wab_ref, cw_ref, cb_ref, wo_ref, st_ref, o_ref, tail_ref,
                xn_scr, halo_scr, *, tm, t, tf):
    i = pl.program_id(0)
    c = pl.program_id(1)

    @pl.when(c == 0)
    def _():
        x = x_ref[...]
        xn_scr[...] = (_rms_rows(x) * g_ref[...]).astype(BF16)
        o_ref[...] = x

    tail_rows = 8
    if t >= tm:
        @pl.when((i % (t // tm)) == 0)
        def _():
            halo_scr[c] = st_ref[0]

        halo = halo_scr[c]
    else:
        st = st_ref[...]
        nseq = tm // t

    ab = _dot(xn_scr[...], wab_ref[...])
    rows8 = lax.broadcasted_iota(jnp.int32, (tail_rows, FF_SUB), 0)
    rows = lax.broadcasted_iota(jnp.int32, (tm, FF_SUB), 0)
    gated = []
    for k in range(tf // FF_SUB):
        cols = slice(k * FF_SUB, (k + 1) * FF_SUB)
        a = ab[:, 2 * k * FF_SUB:(2 * k + 1) * FF_SUB]
        b = ab[:, (2 * k + 1) * FF_SUB:(2 * k + 2) * FF_SUB]
        prev1 = pltpu.roll(a, 1, 0)
        prev2 = pltpu.roll(a, 2, 0)
        if t >= tm:
            h0 = halo[tail_rows - 2:tail_rows - 1, cols]
            h1 = halo[tail_rows - 1:tail_rows, cols]
            head1 = jnp.where(rows8 == 0, h1, prev1[:tail_rows])
            head2 = jnp.where(rows8 == 0, h0, jnp.where(rows8 == 1, h1, prev2[:tail_rows]))
            a1 = jnp.concatenate([head1, prev1[tail_rows:]], axis=0)
            a2 = jnp.concatenate([head2, prev2[tail_rows:]], axis=0)
            last = a[tm - tail_rows:tm, :]
            halo_scr[c, :, cols] = last
            tail_ref[0, :, cols] = last
        else:
            h0 = jnp.broadcast_to(st[:, tail_rows - 2:tail_rows - 1, cols], (nseq, t, FF_SUB)).reshape(tm, FF_SUB)
            h1 = jnp.broadcast_to(st[:, tail_rows - 1:tail_rows, cols], (nseq, t, FF_SUB)).reshape(tm, FF_SUB)
            rmod = rows & (t - 1)
            a1 = jnp.where(rmod == 0, h1, prev1)
            a2 = jnp.where(rmod == 0, h0, jnp.where(rmod == 1, h1, prev2))
            tail_ref[:, :, cols] = a.reshape(nseq, t, FF_SUB)[:, t - tail_rows:, :]
        ac = cw_ref[2:3, cols] * a + cw_ref[1:2, cols] * a1 + cw_ref[0:1, cols] * a2 + cb_ref[:, cols]
        gated.append((ac * (1.0 / (1.0 + jnp.exp(-ac))) * b).astype(BF16))
    o_ref[...] += _dot(jnp.concatenate(gated, axis=1), wo_ref[...])


def _ffn(x, t, g, wab, cw, cb, wo, st):
    n, d = x.shape
    ffp = wo.shape[0]
    tf = min(FF_BLOCK, ffp)
    tm = min(ROW_BLOCK, n)
    assert ffp % tf == 0 and tf % FF_SUB == 0 and n % tm == 0 and (t % tm == 0 or tm % t == 0)
    assert t & (t - 1) == 0
    nseq = n // t
    if t >= tm:
        per = t // tm
        st_spec = pl.BlockSpec((1, 8, tf), lambda i, c: (i // per, 0, c))
        tail_spec = pl.BlockSpec((1, 8, tf), lambda i, c: (i, 0, c))
    else:
        per = 1
        st_spec = tail_spec = pl.BlockSpec((tm // t, 8, tf), lambda i, c: (i, 0, c))
    x_new, tail = pl.pallas_call(
        functools.partial(_ffn_kernel, tm=tm, t=t, tf=tf),
        grid=(n // tm, ffp // tf),
        in_specs=[pl.BlockSpec((tm, d), lambda i, c: (i, 0)),
                  pl.BlockSpec((1, d), lambda i, c: (0, 0)),
                  pl.BlockSpec((d, 2 * tf), lambda i, c: (0, c)),
                  pl.BlockSpec((8, tf), lambda i, c: (0, c)),
                  pl.BlockSpec((1, tf), lambda i, c: (0, c)),
                  pl.BlockSpec((tf, d), lambda i, c: (c, 0)),
                  st_spec],
        out_specs=[pl.BlockSpec((tm, d), lambda i, c: (i, 0)), tail_spec],
        out_shape=[jax.ShapeDtypeStruct((n, d), F32),
                   jax.ShapeDtypeStruct((nseq * per, 8, ffp), F32)],
        scratch_shapes=[pltpu.VMEM((tm, d), BF16),
                        pltpu.VMEM((ffp // tf, 8, tf), F32)],
        compiler_params=_cparams(2),
        name="ffn",
    )(x, g, wab, cw, cb, wo, st)
    return x_new, tail.reshape(nseq, per, 8, ffp)[:, per - 1]


def _norm_kernel(x_ref, g_ref, o_ref):
    o_ref[...] = _rms_rows(x_ref[...]) * g_ref[...]


def _final_norm(x, g):
    n, d = x.shape
    tm = min(ROW_BLOCK, n)
    return pl.pallas_call(
        _norm_kernel,
        grid=(n // tm,),
        in_specs=[pl.BlockSpec((tm, d), lambda i: (i, 0)), pl.BlockSpec((1, d), lambda i: (0, 0))],
        out_specs=pl.BlockSpec((tm, d), lambda i: (i, 0)),
        out_shape=jax.ShapeDtypeStruct((n, d), F32),
        compiler_params=_cparams(1),
        name="final_norm",
    )(x, g)


def _prep_layer(p, dims):
    d_ssm, hd, hf, d_ff, ffp = dims
    wd, wf = hd * LANES, hf * HD_FOX
    o_qd, o_kd, o_vd = d_ssm, d_ssm + wd, d_ssm + 2 * wd
    o_qf, o_kf, o_vf, o_fg = d_ssm + 3 * wd, d_ssm + 3 * wd + wf, d_ssm + 3 * wd + 2 * wf, d_ssm + 3 * wd + 3 * wf
    w_in = p['w_in']
    w_fg = jnp.pad(w_in[:, o_fg:], ((0, 0), (0, LANES - hf)))
    q = dict(p)
    q['w_in_n'] = jnp.concatenate([w_in[:, o_qd:o_kd], w_in[:, o_qf:o_kf], w_fg, w_in[:, o_vd:o_qf]],
                                  axis=1).astype(BF16)
    q['w_in_t'] = jnp.concatenate([w_in[:, :o_qd], w_in[:, o_kd:o_vd], w_in[:, o_kf:o_fg]],
                                  axis=1).T.astype(BF16)
    q['w_glu'] = p['w_glu'].astype(BF16)
    q['w_o'] = p['w_o'].astype(BF16)
    padc = ((0, 0), (0, ffp - d_ff))
    sub = min(FF_SUB, ffp)
    w_a = jnp.pad(p['w_ffn_in'][:, :d_ff], padc).astype(BF16).reshape(-1, ffp // sub, 1, sub)
    w_b = jnp.pad(p['w_ffn_in'][:, d_ff:], padc).astype(BF16).reshape(-1, ffp // sub, 1, sub)
    q['w_ab'] = jnp.concatenate([w_a, w_b], axis=2).reshape(-1, 2 * ffp)
    q['w_ffn_out'] = jnp.pad(p['w_ffn_out'], ((0, ffp - d_ff), (0, 0))).astype(BF16)
    q['conv_w'] = jnp.pad(p['ffn_conv_w'], ((0, 8 - CONV_W), (0, ffp - d_ff)))
    q['conv_b'] = jnp.pad(p['ffn_conv_b'], (0, ffp - d_ff))[None, :]
    s5_args = (p['ssm_lam_re'], p['ssm_lam_im'], p['ssm_log_dt'], p['ssm_b_re'], p['ssm_b_im'],
               p['ssm_c_re'], p['ssm_c_im'], p['ssm_d'])
    q['s5_seq'] = _s5_params(*s5_args, S5_BLOCK)
    q['s5_short'] = _s5_params(*s5_args, CHUNK)
    q['lam'] = (jnp.exp(jnp.sum(p['diff_lam_q1'] * p['diff_lam_k1']))
                - jnp.exp(jnp.sum(p['diff_lam_q2'] * p['diff_lam_k2'])) + p['lam_init'])
    return q


def _layer(x, bsz, t, li, hist, st, p, dims, bias_tiles, lam_init):
    d_ssm, hd, hf, d_ff, ffp = dims
    kdt_h, vd_h, kft_h, vft_h = hist
    lf_c, sr_c, si_c, cv_c = st
    n, d = x.shape
    wd, wf = hd * LANES, hf * HD_FOX
    u, qd, qf, fg, vd, kdt, kft, vft = _inproj(x, t, p['g_norm_mix'][None, :], p['w_in_n'], p['w_in_t'],
                                               d_ssm, wd, wf)
    gm = p['g_mix_out'][None, :]
    if u.ndim == 4:
        y_ssm, sr, si = _s5_seq(u, sr_c, si_c, p['s5_seq'])
    else:
        y_ssm, sr, si = _s5_short(u, sr_c, si_c, p['s5_short'])
    o_ssm = _glu(y_ssm, p['w_glu'], p['b_glu'][None, :], gm[:, :d_ssm])

    vd = vd.reshape(bsz, t * hd, DV_DIFF)
    o_diff = _diff_attn(qd.reshape(bsz, t, wd), kdt, vd, kdt_h, vd_h, li, p['lam'], p['rel_bias'],
                        bias_tiles, gm[:, d_ssm:d_ssm + wd], lam_init)

    logf, cum = _logf_cum(fg.reshape(bsz, t, LANES), p['fox_b_f'], lf_c)
    o_fox = _fox_attn(qf.reshape(bsz, t, wf), kft, vft, kft_h, vft_h, li, cum, gm[:, d_ssm + wd:])

    x = _oproj(x, o_ssm, o_diff.reshape(n, wd), o_fox.reshape(n, wf), p['w_o'])

    x, tail = _ffn(x, t, p['g_norm_ffn'][None, :], p['w_ab'], p['conv_w'], p['conv_b'],
                   p['w_ffn_out'], cv_c)
    new_st = (kdt, vd, kft, vft, logf, sr, si, tail[:, 8 - (CONV_W - 1):, :d_ff])
    return x, new_st


def kernel(x_prompt, x_sample, cache_diff_k, cache_diff_v, cache_fox_k, cache_fox_v, cache_fox_logf, state_ssm_re, state_ssm_im, state_ffn_conv, g_norm_mix, w_in, ssm_lam_re, ssm_lam_im, ssm_log_dt, ssm_b_re, ssm_b_im, ssm_c_re, ssm_c_im, ssm_d, w_glu, b_glu, diff_lam_q1, diff_lam_k1, diff_lam_q2, diff_lam_k2, rel_bias, fox_b_f, g_mix_out, w_o, g_norm_ffn, w_ffn_in, ffn_conv_w, ffn_conv_b, w_ffn_out, g_final):
    per_layer = {'g_norm_mix': g_norm_mix, 'w_in': w_in, 'ssm_lam_re': ssm_lam_re,
                 'ssm_lam_im': ssm_lam_im, 'ssm_log_dt': ssm_log_dt, 'ssm_b_re': ssm_b_re,
                 'ssm_b_im': ssm_b_im, 'ssm_c_re': ssm_c_re, 'ssm_c_im': ssm_c_im, 'ssm_d': ssm_d,
                 'w_glu': w_glu, 'b_glu': b_glu, 'diff_lam_q1': diff_lam_q1, 'diff_lam_k1': diff_lam_k1,
                 'diff_lam_q2': diff_lam_q2, 'diff_lam_k2': diff_lam_k2, 'fox_b_f': fox_b_f,
                 'g_mix_out': g_mix_out, 'w_o': w_o, 'g_norm_ffn': g_norm_ffn, 'w_ffn_in': w_ffn_in,
                 'ffn_conv_w': ffn_conv_w, 'ffn_conv_b': ffn_conv_b, 'w_ffn_out': w_ffn_out}
    depth = w_in.shape[0]
    bp, tp, d_model = x_prompt.shape
    bs, ts, _ = x_sample.shape
    past = cache_diff_k.shape[2]
    g_ssm = ssm_lam_re.shape[1]
    d_ssm = g_ssm * SSM_GROUP
    hd = cache_diff_k.shape[3]
    hf = cache_fox_k.shape[3]
    wd, wf = hd * LANES, hf * HD_FOX
    d_ff = w_ffn_out.shape[1]
    ffp = -(-d_ff // FF_BLOCK) * FF_BLOCK if d_ff > FF_BLOCK else d_ff
    dims = (d_ssm, hd, hf, d_ff, ffp)

    def conv_state_block(s):
        return jnp.pad(s, ((0, 0), (8 - (CONV_W - 1), 0), (0, ffp - d_ff)))

    hist_sample = (jnp.transpose(cache_diff_k, (0, 1, 3, 4, 5, 2)).reshape(depth, bs, wd, past),
                   cache_diff_v.reshape(depth, bs, past * hd, DV_DIFF),
                   jnp.transpose(cache_fox_k, (0, 1, 3, 4, 2)).reshape(depth, bs, wf, past),
                   jnp.transpose(cache_fox_v, (0, 1, 3, 4, 2)).reshape(depth, bs, wf, past))
    hist_prompt = (None, None, None, None)
    st_prompt = (None, jnp.zeros((bp, g_ssm, P_SSM), F32), jnp.zeros((bp, g_ssm, P_SSM), F32),
                 conv_state_block(jnp.zeros((bp, CONV_W - 1, d_ff), F32)))
    bias_p = _bias_tiles(rel_bias, *_attn_blocks(tp, 0)[:2])
    bias_s = _bias_tiles(rel_bias, *_attn_blocks(ts, past)[:2])

    xp = x_prompt.reshape(bp * tp, d_model)
    xs = x_sample.reshape(bs * ts, d_model)
    outs_p, outs_s = [], []
    for li in range(depth):
        lam_init = 0.8 - 0.6 * math.exp(-0.3 * li)
        p = {name: arr[li] for name, arr in per_layer.items()}
        p['rel_bias'] = rel_bias
        p['lam_init'] = lam_init
        p = _prep_layer(p, dims)
        xp, sp = _layer(xp, bp, tp, li, hist_prompt, st_prompt, p, dims, bias_p, lam_init)
        st_sample = (cache_fox_logf[li], state_ssm_re[li], state_ssm_im[li],
                     conv_state_block(state_ffn_conv[li]))
        xs, ss = _layer(xs, bs, ts, li, hist_sample, st_sample, p, dims, bias_s, lam_init)
        outs_p.append(sp)
        outs_s.append(ss)

    def assemble(outs, bsz, t):
        kdt, vd, kft, vft, logf, sr, si, cv = [jnp.stack([s[i] for s in outs]) for i in range(8)]
        return (jnp.transpose(kdt.reshape(depth, bsz, hd, 2, DQK_DIFF, t), (0, 1, 5, 2, 3, 4)),
                vd.reshape(depth, bsz, t, hd, DV_DIFF),
                jnp.transpose(kft.reshape(depth, bsz, hf, HD_FOX, t), (0, 1, 4, 2, 3)),
                jnp.transpose(vft.reshape(depth, bsz, hf, HD_FOX, t), (0, 1, 4, 2, 3)),
                logf, sr, si, cv)

    y_prompt = _final_norm(xp, g_final[None, :]).reshape(bp, tp, d_model)
    y_sample = _final_norm(xs, g_final[None, :]).reshape(bs, ts, d_model)
    return (y_prompt, y_sample, *assemble(outs_p, bp, tp), *assemble(outs_s, bs, ts))
```

```python
import functools
import math

import numpy as np
import jax
import jax.numpy as jnp
from jax import lax
from jax.experimental import pallas as pl
from jax.experimental.pallas import tpu as pltpu

F32 = jnp.float32
BF16 = jnp.bfloat16

EPS = 1e-6
CHUNK = 64
S5_BLOCK = 128
SSM_GROUP = 16
P_SSM = 64
DQK_DIFF = 64
DV_DIFF = 128
HD_FOX = 64
N_BUCKETS = 32
MAX_DISTANCE = 128
CONV_W = 3
LANES = 128
ATTN_BLOCK = 512
FF_BLOCK = 512
FF_SUB = 256
ROW_BLOCK = 512
FF_ROWS = 1024
V7X_VMEM_LIMIT = 56 * 1024 * 1024
NEG = -1e30
LOG2E = math.log2(math.e)


def _cparams(n_axes):
    return pltpu.CompilerParams(dimension_semantics=("arbitrary",) * n_axes,
                                vmem_limit_bytes=V7X_VMEM_LIMIT)


def _resident(shape):
    nd = len(shape)
    return pl.BlockSpec(shape, lambda *_: (0,) * nd, pipeline_mode=pl.Buffered(1))


def _rms_rows(x):
    return x * lax.rsqrt(jnp.mean(x * x, axis=-1, keepdims=True) + EPS)


def _dot(a, b):
    return jnp.dot(a, b, preferred_element_type=F32)


def _dot_nt(a, b):
    return lax.dot_general(a, b, (((1,), (1,)), ((), ())), preferred_element_type=F32)


def _inproj_kernel(x_ref, g_ref, wn_ref, wt_ref, *refs, d_ssm, wd, wf, ns, u_slabs):
    u_ref, qd_ref, qf_ref, fg_ref, vd_ref, kdt_ref, kft_ref, vft_ref = refs[-8:]
    tm = x_ref.shape[0]
    xn = (_rms_rows(x_ref[...]) * g_ref[...]).astype(BF16)
    c = 0
    qd_ref[...] = (_dot(xn, wn_ref[:, c:c + wd]) * (DQK_DIFF ** -0.5 * LOG2E)).astype(BF16)
    c += wd
    qf_ref[...] = (_dot(xn, wn_ref[:, c:c + wf]) * (HD_FOX ** -0.5 * LOG2E)).astype(BF16)
    c += wf
    fg_ref[...] = _dot(xn, wn_ref[:, c:c + LANES])
    c += LANES
    vd = _dot(xn, wn_ref[:, c:c + wd])
    nh = wd // DV_DIFF
    for h in range(nh):
        vd_ref[0, pl.ds(h, tm, stride=nh), :] = vd[:, h * DV_DIFF:(h + 1) * DV_DIFF]
    ts = tm // ns
    zt = _dot_nt(wt_ref[0:d_ssm, :], xn)
    if u_slabs:
        for k in range(u_slabs):
            u_ref[k, 0] = zt[:, k * S5_BLOCK:(k + 1) * S5_BLOCK]
    else:
        for s in range(ns):
            u_ref[0, s] = zt[:, s * ts:(s + 1) * ts]
    r = d_ssm
    for o_ref, width in ((kdt_ref, wd), (kft_ref, wf), (vft_ref, wf)):
        zt = _dot_nt(wt_ref[r:r + width, :], xn)
        for s in range(ns):
            o_ref[0, s] = zt[:, s * ts:(s + 1) * ts]
        r += width


def _inproj(x, t, g, wn, wt, d_ssm, wd, wf, slot, nslots, stacks):
    n, d = x.shape
    tm = min(ROW_BLOCK, n)
    nseq = n // t
    nh = wd // DV_DIFF
    assert n % tm == 0 and (t % tm == 0 or tm % t == 0)
    if t >= tm:
        per, ns = t // tm, 1
        tspec = lambda width: pl.BlockSpec((1, 1, width, tm), lambda i: (slot, i // per, 0, i % per))
        u_slabs = tm // S5_BLOCK
        u_spec = pl.BlockSpec((u_slabs, 1, d_ssm, S5_BLOCK), lambda i: (i % per, i // per, 0, 0))
        u_shape = (t // S5_BLOCK, nseq, d_ssm, S5_BLOCK)
    else:
        ns = tm // t
        tspec = lambda width: pl.BlockSpec((1, ns, width, t), lambda i: (slot, i, 0, 0))
        u_slabs = 0
        u_spec = pl.BlockSpec((1, ns, d_ssm, t), lambda i: (0, i, 0, 0))
        u_shape = (1, nseq, d_ssm, t)
    row = lambda width: pl.BlockSpec((tm, width), lambda i: (i, 0))
    in_specs = [row(d), _resident((1, d)), _resident(wn.shape), _resident(wt.shape)]
    args = [x, g, wn, wt]
    aliases = {}
    if stacks is not None:
        in_specs += [pl.BlockSpec(memory_space=pl.ANY)] * len(stacks)
        aliases = {len(args) + k: 4 + k for k in range(len(stacks))}
        args += list(stacks)
    outs = pl.pallas_call(
        functools.partial(_inproj_kernel, d_ssm=d_ssm, wd=wd, wf=wf, ns=ns, u_slabs=u_slabs),
        grid=(n // tm,),
        in_specs=in_specs,
        out_specs=[u_spec, row(wd), row(wf), row(LANES),
                   pl.BlockSpec((1, tm * nh, DV_DIFF), lambda i: (slot, i, 0)),
                   tspec(wd), tspec(wf), tspec(wf)],
        out_shape=[jax.ShapeDtypeStruct(u_shape, F32),
                   jax.ShapeDtypeStruct((n, wd), BF16),
                   jax.ShapeDtypeStruct((n, wf), BF16),
                   jax.ShapeDtypeStruct((n, LANES), F32),
                   jax.ShapeDtypeStruct((nslots, n * nh, DV_DIFF), F32),
                   jax.ShapeDtypeStruct((nslots, nseq, wd, t), F32),
                   jax.ShapeDtypeStruct((nslots, nseq, wf, t), F32),
                   jax.ShapeDtypeStruct((nslots, nseq, wf, t), F32)],
        input_output_aliases=aliases,
        compiler_params=_cparams(1),
        name="inproj",
    )(*args)
    return outs[0], outs[1], outs[2], outs[3], tuple(outs[4:])


def _s5_block_ops(u, kt_ref, p_ref, q_ref, a_ref, d_ref, h0_ref, ht_ref, m_scr, v_scr, hin_scr,
                  nb, bsz, blk, emit):
    width = SSM_GROUP * blk
    lane = lax.broadcasted_iota(jnp.int32, (blk, width), 1)
    row = lax.broadcasted_iota(jnp.int32, (blk, width), 0)
    causal = (lane & (blk - 1)) >= row
    for h in range(SSM_GROUP):
        base = jnp.broadcast_to(kt_ref[0, h:h + 1, :], (blk, width))
        shifted = pltpu.roll(base, 0, 1, stride=1, stride_axis=0)
        m_scr[h * blk:(h + 1) * blk, :] = jnp.where(causal, shifted, 0.0).astype(BF16)

    ub = u.astype(BF16)
    emit(_dot(ub, m_scr[...]) + d_ref[0] * u, None)
    v_scr[...] = _dot(ub, p_ref[0])

    a_same = a_ref[0, 0:1, :]
    a_cross = a_ref[0, 1:2, :]
    h = h0_ref[0]
    for c in range(nb):
        hin_scr[c * bsz:(c + 1) * bsz, :] = h
        h = a_same * h + a_cross * pltpu.roll(h, P_SSM, 1) + v_scr[c * bsz:(c + 1) * bsz, :]
    ht_ref[0] = h
    emit(None, _dot(hin_scr[...].astype(BF16), q_ref[0]))


def _s5_kernel(u_ref, kt_ref, p_ref, q_ref, a_ref, d_ref, h0_ref, y_ref, ht_ref,
               m_scr, v_scr, hin_scr, *, nb, bsz, blk):
    def emit(y_local, y_state):
        if y_local is not None:
            y_ref[0] = y_local
        else:
            y_ref[0] += y_state

    _s5_block_ops(u_ref[0], kt_ref, p_ref, q_ref, a_ref, d_ref, h0_ref, ht_ref, m_scr, v_scr, hin_scr,
                  nb, bsz, blk, emit)


def _s5_seq_kernel(u_ref, kt_ref, p_ref, q_ref, a_ref, d_ref, h0_ref, y_ref, ht_ref,
                   m_scr, v_scr, hin_scr, u_scr, y_scr, *, nb, bsz, blk):
    rows = nb * bsz
    for h in range(SSM_GROUP):
        u_scr[:, h * blk:(h + 1) * blk] = u_ref[:, :, h, :].reshape(rows, blk)

    def emit(y_local, y_state):
        if y_local is not None:
            y_scr[...] = y_local
        else:
            y_scr[...] += y_state

    _s5_block_ops(u_scr[...], kt_ref, p_ref, q_ref, a_ref, d_ref, h0_ref, ht_ref, m_scr, v_scr, hin_scr,
                  nb, bsz, blk, emit)
    for h in range(SSM_GROUP):
        y_ref[:, :, h, :] = y_scr[:, h * blk:(h + 1) * blk].reshape(nb, bsz, blk)


def _s5_params(lam_re, lam_im, log_dt, b_re, b_im, c_re, c_im, d_skip, blks):
    hp = lax.Precision.HIGHEST
    g = lam_re.shape[0]
    top = max(blks)
    dt = jnp.exp(log_dt)[:, None]
    k = jnp.arange(0, top + 1, dtype=F32)[:, None, None]
    mag = jnp.exp(lam_re * dt * k)
    pr = mag * jnp.cos(lam_im * dt * k)
    pi = mag * jnp.sin(lam_im * dt * k)
    ar, ai = pr[1], pi[1]
    den = lam_re * lam_re + lam_im * lam_im
    fr = ((ar - 1.0) * lam_re + ai * lam_im) / den
    fi = (ai * lam_re - (ar - 1.0) * lam_im) / den
    bbr = fr[..., None] * b_re - fi[..., None] * b_im
    bbi = fr[..., None] * b_im + fi[..., None] * b_re
    car = c_re[None] * pr[:, :, None, :] - c_im[None] * pi[:, :, None, :]
    cai = c_re[None] * pi[:, :, None, :] + c_im[None] * pr[:, :, None, :]
    kt_top = (jnp.einsum('kgap,gph->ghak', car[:top], bbr, precision=hp)
              - jnp.einsum('kgap,gph->ghak', cai[:top], bbi, precision=hp))
    out = {}
    for blk in blks:
        kt = kt_top[..., :blk].reshape(g, SSM_GROUP, SSM_GROUP * blk)
        prr = pr[blk - 1::-1][:blk]
        pir = pi[blk - 1::-1][:blk]
        p_re = prr[:, :, :, None] * bbr[None] - pir[:, :, :, None] * bbi[None]
        p_im = prr[:, :, :, None] * bbi[None] + pir[:, :, :, None] * bbr[None]
        pmat = jnp.concatenate([jnp.transpose(p_re, (1, 3, 0, 2)), jnp.transpose(p_im, (1, 3, 0, 2))],
                               axis=-1).reshape(g, SSM_GROUP * blk, 2 * P_SSM)
        q_re = jnp.transpose(car[1:blk + 1], (1, 3, 2, 0))
        q_im = -jnp.transpose(cai[1:blk + 1], (1, 3, 2, 0))
        qmat = jnp.concatenate([q_re, q_im], axis=1).reshape(g, 2 * P_SSM, SSM_GROUP * blk)
        a_blk = jnp.stack([jnp.concatenate([pr[blk], pr[blk]], -1),
                           jnp.concatenate([-pi[blk], pi[blk]], -1)], axis=1)
        dvec = jnp.repeat(d_skip, blk, axis=-1)[:, None, :]
        out[blk] = (kt, pmat.astype(BF16), qmat.astype(BF16), a_blk, dvec)
    return out


def _s5_call(kern, u_arr, u_spec, sp, h_re, h_im, nb, bsz, blk, extra_scratch):
    kt, pmat, qmat, a_blk, dvec = sp
    g = kt.shape[0]
    width = SSM_GROUP * blk
    rows = nb * bsz
    h0 = jnp.transpose(jnp.concatenate([h_re, h_im], axis=-1), (1, 0, 2))
    y, ht = pl.pallas_call(
        functools.partial(kern, nb=nb, bsz=bsz, blk=blk),
        grid=(g,),
        in_specs=[u_spec,
                  pl.BlockSpec((1, SSM_GROUP, width), lambda i: (i, 0, 0)),
                  pl.BlockSpec((1, width, 2 * P_SSM), lambda i: (i, 0, 0)),
                  pl.BlockSpec((1, 2 * P_SSM, width), lambda i: (i, 0, 0)),
                  pl.BlockSpec((1, 2, 2 * P_SSM), lambda i: (i, 0, 0)),
                  pl.BlockSpec((1, 1, width), lambda i: (i, 0, 0)),
                  pl.BlockSpec((1, bsz, 2 * P_SSM), lambda i: (i, 0, 0))],
        out_specs=[u_spec, pl.BlockSpec((1, bsz, 2 * P_SSM), lambda i: (i, 0, 0))],
        out_shape=[jax.ShapeDtypeStruct(u_arr.shape, F32),
                   jax.ShapeDtypeStruct((g, bsz, 2 * P_SSM), F32)],
        scratch_shapes=[pltpu.VMEM((width, width), BF16),
                        pltpu.VMEM((rows, 2 * P_SSM), F32),
                        pltpu.VMEM((rows, 2 * P_SSM), F32)] + extra_scratch,
        compiler_params=_cparams(1),
        name="s5",
    )(u_arr, kt, pmat, qmat, a_blk, dvec, h0)
    ht = jnp.transpose(ht, (1, 0, 2))
    return y, ht[..., :P_SSM], ht[..., P_SSM:]


def _s5_short(ut, h_re, h_im, sp):
    bsz, dssm, t = ut.shape
    g = dssm // SSM_GROUP
    nb = t // CHUNK
    width = SSM_GROUP * CHUNK
    rows = nb * bsz
    u = jnp.transpose(ut.reshape(bsz, g, SSM_GROUP, nb, CHUNK), (1, 3, 0, 2, 4)).reshape(g, rows, width)
    y, sr, si = _s5_call(_s5_kernel, u, pl.BlockSpec((1, rows, width), lambda i: (i, 0, 0)),
                         sp, h_re, h_im, nb, bsz, CHUNK, [])
    y = jnp.transpose(y.reshape(g, nb, bsz, SSM_GROUP, CHUNK), (2, 1, 4, 0, 3)).reshape(bsz * t, dssm)
    return y, sr, si


def _s5_seq(u4, h_re, h_im, sp):
    nb, bsz, dssm, blk = u4.shape
    rows, width = nb * bsz, SSM_GROUP * blk
    spec = pl.BlockSpec((nb, bsz, SSM_GROUP, blk), lambda i: (0, 0, i, 0))
    return _s5_call(_s5_seq_kernel, u4, spec, sp, h_re, h_im, nb, bsz, blk,
                    [pltpu.VMEM((rows, width), F32), pltpu.VMEM((rows, width), F32)])


def _glu_math(y, w_ref, b_ref, gm_ref):
    g = 0.5 * y * (1.0 + jnp.tanh(math.sqrt(2.0 / math.pi) * (y + 0.044715 * (y * y * y))))
    z = _dot(g.astype(BF16), w_ref[...]) + b_ref[...]
    yy = g * (1.0 / (1.0 + jnp.exp(-z)))
    return (_rms_rows(yy) * gm_ref[...]).astype(BF16)


def _glu_kernel(y_ref, w_ref, b_ref, gm_ref, o_ref):
    o_ref[...] = _glu_math(y_ref[...], w_ref, b_ref, gm_ref)


def _glu_seq_kernel(y_ref, w_ref, b_ref, gm_ref, o_ref, y_scr):
    blk = y_ref.shape[-1]
    for k in range(y_ref.shape[0]):
        y_scr[k * blk:(k + 1) * blk, :] = jnp.transpose(y_ref[k, 0])
    o_ref[...] = _glu_math(y_scr[...], w_ref, b_ref, gm_ref)


def _glu(y, w, b, gm):
    if y.ndim == 2:
        n, d = y.shape
        tm = min(ROW_BLOCK, n)
        kern, y_spec, scratch = _glu_kernel, pl.BlockSpec((tm, d), lambda i: (i, 0)), []
    else:
        nb, bsz, d, blk = y.shape
        n, tm = nb * bsz * blk, ROW_BLOCK
        per = nb * blk // tm
        kern = _glu_seq_kernel
        y_spec = pl.BlockSpec((tm // blk, 1, d, blk), lambda i: (i % per, i // per, 0, 0))
        scratch = [pltpu.VMEM((tm, d), F32)]
    return pl.pallas_call(
        kern,
        grid=(n // tm,),
        in_specs=[y_spec, _resident(w.shape), _resident((1, d)), _resident((1, d))],
        out_specs=pl.BlockSpec((tm, d), lambda i: (i, 0)),
        out_shape=jax.ShapeDtypeStruct((n, d), BF16),
        scratch_shapes=scratch,
        compiler_params=_cparams(1),
        name="glu",
    )(y, w, b, gm)


def _t5_bucket_np(rel):
    nb = N_BUCKETS // 2
    max_exact = nb // 2
    n = np.abs(rel)
    large = max_exact + (np.log(np.maximum(n, 1).astype(np.float32) / np.float32(max_exact))
                         / np.float32(math.log(MAX_DISTANCE / max_exact))
                         * np.float32(nb - max_exact)).astype(np.int32)
    large = np.minimum(large, nb - 1)
    return (np.where(rel > 0, nb, 0) + np.where(n < max_exact, n, large)).astype(np.int32)


def _bucket_thresholds():
    dist = np.arange(0, 4 * MAX_DISTANCE, dtype=np.int32)
    bk = _t5_bucket_np(-dist)
    assert np.all(np.diff(bk) >= 0) and bk[-1] == N_BUCKETS // 2 - 1
    assert np.array_equal(_t5_bucket_np(dist[1:]), bk[1:] + N_BUCKETS // 2)
    return [int(np.argmax(bk >= k)) for k in range(N_BUCKETS // 2)]


def _attn_blocks(t, past):
    tq = min(ATTN_BLOCK, t)
    nq = t // tq
    assert t % tq == 0 and tq % CHUNK == 0
    if nq == 1:
        tk = past if past else tq
    else:
        assert past == 0
        tk = tq
    return tq, tk, nq


def _online_update(m, l, acc, s, shift, pv):
    m_new = jnp.maximum(m, jnp.max(s, axis=-1, keepdims=True) + shift)
    alpha = jnp.exp2(m - m_new)
    p = jnp.exp2(s - (m_new - shift))
    l = alpha * l + jnp.sum(p, axis=-1, keepdims=True)
    acc = alpha * acc + pv(p.astype(BF16))
    return m_new, l, acc


def _softmax_init(tq, width):
    return (jnp.full((tq, 1), NEG, F32), jnp.zeros((tq, 1), F32), jnp.zeros((tq, width), F32))


def _bias_kernel(rb_ref, near_ref, diag_ref, *, nh, thr):
    h = pl.program_id(0)
    half = N_BUCKETS // 2

    def tile(shape, col_off):
        r = lax.broadcasted_iota(jnp.int32, shape, 0)
        c = lax.broadcasted_iota(jnp.int32, shape, 1)
        rel = c + col_off - r
        dist = jnp.abs(rel)
        back = jnp.full(shape, rb_ref[h], F32)
        fwd = jnp.full(shape, rb_ref[half * nh + h], F32)
        for k in range(1, half):
            far = dist >= thr[k]
            back = jnp.where(far, rb_ref[k * nh + h], back)
            fwd = jnp.where(far, rb_ref[(half + k) * nh + h], fwd)
        return jnp.where(rel > 0, fwd, back) * LOG2E, r, c

    near, _, _ = tile(near_ref.shape[1:], -near_ref.shape[2])
    near_ref[0] = near
    diag, r, c = tile(diag_ref.shape[1:], 0)
    diag_ref[0] = jnp.where((c // CHUNK) <= (r // CHUNK), diag, NEG)


def _bias_tiles(rel_bias, tq, tk):
    nh = rel_bias.shape[1]
    return pl.pallas_call(
        functools.partial(_bias_kernel, nh=nh, thr=_bucket_thresholds()),
        grid=(nh,),
        in_specs=[pl.BlockSpec(memory_space=pltpu.SMEM)],
        out_specs=[pl.BlockSpec((1, tq, tk), lambda h: (h, 0, 0)),
                   pl.BlockSpec((1, tq, tq), lambda h: (h, 0, 0))],
        out_shape=[jax.ShapeDtypeStruct((nh, tq, tk), F32),
                   jax.ShapeDtypeStruct((nh, tq, tq), F32)],
        compiler_params=_cparams(1),
        name="rel_bias_tiles",
    )(rel_bias.reshape(-1))


def _diff_kernel(lam_ref, rb_ref, q_ref, kt_ref, v_ref, *rest, t, past, tq, tk, nh, out_scale):
    if past:
        kth_ref, vh_ref = rest[0], rest[1]
        rest = rest[2:]
    bn_ref, bd_ref, gm_ref, o_ref, kt_scr, v_scr = rest
    h = pl.program_id(1)
    qi = pl.program_id(2)
    nq = t // tq

    @pl.when(qi == 0)
    def _():
        if past:
            kt_scr[:, 0:past] = kth_ref[0, 0].astype(BF16)
            v_scr[0:past, :] = vh_ref[0, 0, pl.ds(h, past, stride=nh), :].astype(BF16)
        kt_scr[:, past:past + t] = kt_ref[0, 0].astype(BF16)
        v_scr[past:past + t, :] = v_ref[0, 0, pl.ds(h, t, stride=nh), :].astype(BF16)

    q = q_ref[0]
    lane = lax.broadcasted_iota(jnp.int32, q.shape, 1)
    q1 = jnp.where(lane < DQK_DIFF, q, jnp.zeros_like(q))
    q2 = jnp.where(lane >= DQK_DIFF, q, jnp.zeros_like(q))
    far_bias = rb_ref[(N_BUCKETS // 2 - 1) * nh + h] * LOG2E

    def block(carry, off, width, bias_ref):
        kt = kt_scr[:, pl.ds(off, width)]
        vb = v_scr[pl.ds(off, width), :]
        c1, c2 = carry
        pv = lambda p: _dot(p, vb)
        s1, s2 = _dot(q1, kt), _dot(q2, kt)
        if bias_ref is None:
            shift = far_bias
        else:
            shift = 0.0
            s1, s2 = s1 + bias_ref[0], s2 + bias_ref[0]
        return _online_update(*c1, s1, shift, pv), _online_update(*c2, s2, shift, pv)

    carry = (_softmax_init(tq, DV_DIFF), _softmax_init(tq, DV_DIFF))
    if nq == 1:
        if past:
            carry = block(carry, 0, past, bn_ref)
        d0 = past
    else:
        carry = lax.fori_loop(
            0, qi - 1, lambda j, cr: block(cr, pl.multiple_of(j * tk, tk), tk, None), carry)
        carry = lax.cond(qi >= 1,
                         lambda cr: block(cr, pl.multiple_of((qi - 1) * tk, tk), tk, bn_ref),
                         lambda cr: cr, carry)
        d0 = pl.multiple_of(qi * tq, tq)
    carry = block(carry, d0, tq, bd_ref)

    (_, l1, a1), (_, l2, a2) = carry
    o = a1 / l1 - lam_ref[0] * (a2 / l2)
    o_ref[0] = (_rms_rows(o) * out_scale * gm_ref[...]).astype(BF16)


def _diff_attn(q, kt, v, slot, kth, vh, li, lam, rel_bias, bias_tiles, gm, lam_init):
    bsz, t, hw = q.shape
    nh = hw // LANES
    past = 0 if kth is None else kth.shape[-1]
    tq, tk, nq = _attn_blocks(t, past)
    bias_near, bias_diag = bias_tiles
    smem = pl.BlockSpec(memory_space=pltpu.SMEM)
    args = [lam.reshape(1), rel_bias.reshape(-1), q, kt, v]
    in_specs = [smem, smem,
                pl.BlockSpec((1, tq, LANES), lambda b, h, i: (b, i, h)),
                pl.BlockSpec((1, 1, LANES, t), lambda b, h, i: (slot, b, h, 0)),
                pl.BlockSpec((1, 1, t * nh, LANES), lambda b, h, i: (slot, b, 0, 0))]
    if past:
        args += [kth, vh]
        in_specs += [pl.BlockSpec((1, 1, LANES, past), lambda b, h, i: (li, b, h, 0)),
                     pl.BlockSpec((1, 1, past * nh, LANES), lambda b, h, i: (li, b, 0, 0))]
    args += [bias_near, bias_diag, gm]
    in_specs += [pl.BlockSpec((1, tq, tk), lambda b, h, i: (h, 0, 0)),
                 pl.BlockSpec((1, tq, tq), lambda b, h, i: (h, 0, 0)),
                 pl.BlockSpec((1, LANES), lambda b, h, i: (0, h))]
    return pl.pallas_call(
        functools.partial(_diff_kernel, t=t, past=past, tq=tq, tk=tk, nh=nh, out_scale=1.0 - lam_init),
        grid=(bsz, nh, nq),
        in_specs=in_specs,
        out_specs=pl.BlockSpec((1, tq, LANES), lambda b, h, i: (b, i, h)),
        out_shape=jax.ShapeDtypeStruct((bsz, t, hw), BF16),
        scratch_shapes=[pltpu.VMEM((LANES, past + t), BF16),
                        pltpu.VMEM((past + t, LANES), BF16)],
        compiler_params=_cparams(3),
        name="diff_attn",
    )(*args)


def _logf_kernel(fg_ref, bf_ref, *rest, t, past, nh):
    if past:
        past_ref, logf_ref, cum_ref = rest
    else:
        logf_ref, cum_ref = rest
    x = fg_ref[0][:, :nh] + bf_ref[...]
    lf = -(jnp.maximum(-x, 0.0) + jnp.log1p(jnp.exp(-jnp.abs(x))))
    logf_ref[0] = lf

    def tri(n):
        r = lax.broadcasted_iota(jnp.int32, (n, n), 0)
        c = lax.broadcasted_iota(jnp.int32, (n, n), 1)
        return (r >= c).astype(F32)

    carry = jnp.zeros((1, nh), F32)
    pos = 0
    for src, length in ((None, past), (lf, t)):
        for r0 in range(0, length, LANES):
            n = min(LANES, length - r0)
            blk = past_ref[0, r0:r0 + n, :] if src is None else src[r0:r0 + n, :]
            cs = jnp.dot(tri(n), blk, preferred_element_type=F32,
                         precision=lax.Precision.HIGHEST) + carry
            cum_ref[0, pos:pos + n, :] = cs
            carry = cs[n - 1:n, :]
            pos += n


def _logf_cum(fg, b_f, past_logf):
    bsz, t, _ = fg.shape
    nh = b_f.shape[-1]
    past = 0 if past_logf is None else past_logf.shape[1]
    args = [fg, b_f.reshape(1, nh)]
    in_specs = [pl.BlockSpec((1, t, LANES), lambda b: (b, 0, 0)),
                pl.BlockSpec((1, nh), lambda b: (0, 0))]
    if past:
        args.append(past_logf)
        in_specs.append(pl.BlockSpec((1, past, nh), lambda b: (b, 0, 0)))
    return pl.pallas_call(
        functools.partial(_logf_kernel, t=t, past=past, nh=nh),
        grid=(bsz,),
        in_specs=in_specs,
        out_specs=[pl.BlockSpec((1, t, nh), lambda b: (b, 0, 0)),
                   pl.BlockSpec((1, past + t, nh), lambda b: (b, 0, 0))],
        out_shape=[jax.ShapeDtypeStruct((bsz, t, nh), F32),
                   jax.ShapeDtypeStruct((bsz, past + t, nh), F32)],
        compiler_params=_cparams(1),
        name="logf_cum",
    )(*args)


def _fox_kernel(q_ref, kt_ref, vt_ref, *rest, t, past, tq, tk):
    if past:
        kth_ref, vth_ref = rest[0], rest[1]
        rest = rest[2:]
    cq_ref, ck_ref, gm_ref, o_ref, kt_scr, vt_scr = rest
    hp = pl.program_id(1)
    qi = pl.program_id(2)
    nq = t // tq

    @pl.when(qi == 0)
    def _():
        if past:
            kt_scr[:, 0:past] = kth_ref[0, 0].astype(BF16)
            vt_scr[:, 0:past] = vth_ref[0, 0].astype(BF16)
        kt_scr[:, past:past + t] = kt_ref[0, 0].astype(BF16)
        vt_scr[:, past:past + t] = vt_ref[0, 0].astype(BF16)

    q = q_ref[0]
    lane = lax.broadcasted_iota(jnp.int32, q.shape, 1)
    cq_blk = cq_ref[0]
    head_lane = lax.broadcasted_iota(jnp.int32, cq_blk.shape, 1)
    r = lax.broadcasted_iota(jnp.int32, (tq, tq), 0)
    c = lax.broadcasted_iota(jnp.int32, (tq, tq), 1)
    causal = c <= r

    def head_row(blk, head):
        rid = lax.broadcasted_iota(jnp.int32, blk.shape, 0)
        return jnp.sum(jnp.where(rid == head, blk, 0.0), axis=0, keepdims=True)

    qh, cq = [], []
    for hh in range(2):
        in_half = (lane >= hh * HD_FOX) & (lane < (hh + 1) * HD_FOX)
        qh.append(jnp.where(in_half, q, jnp.zeros_like(q)))
        cq.append(jnp.sum(jnp.where(head_lane == 2 * hp + hh, cq_blk, 0.0), axis=-1, keepdims=True) * LOG2E)

    def block(carry, off, width, mask):
        kt = kt_scr[:, pl.ds(off, width)]
        vt = vt_scr[:, pl.ds(off, width)]
        ck_all = ck_ref[0, :, pl.ds(off, width)] * LOG2E
        pv = lambda p: _dot_nt(p, vt)
        out = []
        for hh in range(2):
            s = _dot(qh[hh], kt) - head_row(ck_all, 2 * hp + hh)
            if mask is not None:
                s = jnp.where(mask, s, NEG)
            out.append(_online_update(*carry[hh], s, cq[hh], pv))
        return tuple(out)

    carry = (_softmax_init(tq, LANES), _softmax_init(tq, LANES))
    if nq == 1:
        if past:
            carry = block(carry, 0, past, None)
        d0 = past
    else:
        carry = lax.fori_loop(
            0, qi, lambda j, cr: block(cr, pl.multiple_of(j * tk, tk), tk, None), carry)
        d0 = pl.multiple_of(qi * tq, tq)
    carry = block(carry, d0, tq, causal)

    (_, l0, a0), (_, l1, a1) = carry
    o = jnp.where(lane < HD_FOX, a0 / l0, a1 / l1)
    sq = o * o
    ms0 = jnp.sum(jnp.where(lane < HD_FOX, sq, 0.0), axis=-1, keepdims=True) / HD_FOX
    ms1 = jnp.sum(jnp.where(lane >= HD_FOX, sq, 0.0), axis=-1, keepdims=True) / HD_FOX
    inv = jnp.where(lane < HD_FOX, lax.rsqrt(ms0 + EPS), lax.rsqrt(ms1 + EPS))
    o_ref[0] = (o * inv * gm_ref[...]).astype(BF16)


def _fox_attn(q, kt, vt, slot, kth, vth, li, cum, gm):
    bsz, t, hw = q.shape
    npair = hw // LANES
    past = 0 if kth is None else kth.shape[-1]
    nh = cum.shape[-1]
    tq, tk, nq = _attn_blocks(t, past)
    cq = cum[:, past:, :]
    ck = jnp.swapaxes(cum, 1, 2)
    kv_spec = pl.BlockSpec((1, 1, LANES, t), lambda b, h, i: (slot, b, h, 0))
    args = [q, kt, vt]
    in_specs = [pl.BlockSpec((1, tq, LANES), lambda b, h, i: (b, i, h)), kv_spec, kv_spec]
    if past:
        h_spec = pl.BlockSpec((1, 1, LANES, past), lambda b, h, i: (li, b, h, 0))
        args += [kth, vth]
        in_specs += [h_spec, h_spec]
    args += [cq, ck, gm]
    in_specs += [pl.BlockSpec((1, tq, nh), lambda b, h, i: (b, i, 0)),
                 pl.BlockSpec((1, nh, past + t), lambda b, h, i: (b, 0, 0)),
                 pl.BlockSpec((1, LANES), lambda b, h, i: (0, h))]
    return pl.pallas_call(
        functools.partial(_fox_kernel, t=t, past=past, tq=tq, tk=tk),
        grid=(bsz, npair, nq),
        in_specs=in_specs,
        out_specs=pl.BlockSpec((1, tq, LANES), lambda b, h, i: (b, i, h)),
        out_shape=jax.ShapeDtypeStruct((bsz, t, hw), BF16),
        scratch_shapes=[pltpu.VMEM((LANES, past + t), BF16),
                        pltpu.VMEM((LANES, past + t), BF16)],
        compiler_params=_cparams(3),
        name="fox_attn",
    )(*args)


def _oproj_kernel(x_ref, a_ref, b_ref, c_ref, w_ref, o_ref, *, d1, d2):
    acc = _dot(a_ref[...], w_ref[0:d1, :])
    acc += _dot(b_ref[...], w_ref[d1:d2, :])
    acc += _dot(c_ref[...], w_ref[d2:, :])
    o_ref[...] = x_ref[...] + acc


def _oproj(x, a, b, c, w):
    n, d = x.shape
    tm = min(ROW_BLOCK, n)
    d1 = a.shape[1]
    d2 = d1 + b.shape[1]
    row = lambda width: pl.BlockSpec((tm, width), lambda i: (i, 0))
    return pl.pallas_call(
        functools.partial(_oproj_kernel, d1=d1, d2=d2),
        grid=(n // tm,),
        in_specs=[row(d), row(a.shape[1]), row(b.shape[1]), row(c.shape[1]), _resident(w.shape)],
        out_specs=row(d),
        out_shape=jax.ShapeDtypeStruct((n, d), F32),
        compiler_params=_cparams(1),
        name="oproj",
    )(x, a, b, c, w)


def _ffn_kernel(x_ref, g_ref, wa_ref, wb_ref, cw_ref, cb_ref, wo_ref, st_ref, o_ref, tail_ref,
                xn_scr, halo_scr, *, tm, t, tf):
    i = pl.program_id(0)
    c = pl.program_id(1)

    @pl.when(c == 0)
    def _():
        x = x_ref[...]
        xn_scr[...] = (_rms_rows(x) * g_ref[...]).astype(BF16)
        o_ref[...] = x

    tail_rows = 8
    if t >= tm:
        @pl.when((i % (t // tm)) == 0)
        def _():
            halo_scr[c] = st_ref[0]

        halo = halo_scr[c]
    else:
        st = st_ref[...]
        nseq = tm // t

    xn = xn_scr[...]
    rows8 = lax.broadcasted_iota(jnp.int32, (tail_rows, FF_SUB), 0)
    rows = lax.broadcasted_iota(jnp.int32, (tm, FF_SUB), 0)
    gated = []
    for k in range(tf // FF_SUB):
        cols = slice(k * FF_SUB, (k + 1) * FF_SUB)
        a = _dot(xn, wa_ref[:, cols])
        b = _dot(xn, wb_ref[:, cols])
        prev1 = pltpu.roll(a, 1, 0)
        prev2 = pltpu.roll(a, 2, 0)
        if t >= tm:
            h0 = halo[tail_rows - 2:tail_rows - 1, cols]
            h1 = halo[tail_rows - 1:tail_rows, cols]
            head1 = jnp.where(rows8 == 0, h1, prev1[:tail_rows])
            head2 = jnp.where(rows8 == 0, h0, jnp.where(rows8 == 1, h1, prev2[:tail_rows]))
            a1 = jnp.concatenate([head1, prev1[tail_rows:]], axis=0)
            a2 = jnp.concatenate([head2, prev2[tail_rows:]], axis=0)
            last = a[tm - tail_rows:tm, :]
            halo_scr[c, :, cols] = last
            tail_ref[0, :, cols] = last
        else:
            h0 = jnp.broadcast_to(st[:, tail_rows - 2:tail_rows - 1, cols], (nseq, t, FF_SUB)).reshape(tm, FF_SUB)
            h1 = jnp.broadcast_to(st[:, tail_rows - 1:tail_rows, cols], (nseq, t, FF_SUB)).reshape(tm, FF_SUB)
            rmod = rows & (t - 1)
            a1 = jnp.where(rmod == 0, h1, prev1)
            a2 = jnp.where(rmod == 0, h0, jnp.where(rmod == 1, h1, prev2))
            tail_ref[:, :, cols] = a.reshape(nseq, t, FF_SUB)[:, t - tail_rows:, :]
        ac = cw_ref[2:3, cols] * a + cw_ref[1:2, cols] * a1 + cw_ref[0:1, cols] * a2 + cb_ref[:, cols]
        gated.append((ac * (1.0 / (1.0 + jnp.exp(-ac))) * b).astype(BF16))
    o_ref[...] += _dot(jnp.concatenate(gated, axis=1), wo_ref[...])


def _ffn(x, t, g, wa, wb, cw, cb, wo, st):
    n, d = x.shape
    ffp = wo.shape[0]
    tf = min(FF_BLOCK, ffp)
    tm = min(FF_ROWS, n)
    assert ffp % tf == 0 and tf % FF_SUB == 0 and n % tm == 0 and (t % tm == 0 or tm % t == 0)
    assert t & (t - 1) == 0
    nseq = n // t
    if t >= tm:
        per = t // tm
        st_spec = pl.BlockSpec((1, 8, tf), lambda i, c: (i // per, 0, c))
        tail_spec = pl.BlockSpec((1, 8, tf), lambda i, c: (i, 0, c))
    else:
        per = 1
        st_spec = tail_spec = pl.BlockSpec((tm // t, 8, tf), lambda i, c: (i, 0, c))
    x_new, tail = pl.pallas_call(
        functools.partial(_ffn_kernel, tm=tm, t=t, tf=tf),
        grid=(n // tm, ffp // tf),
        in_specs=[pl.BlockSpec((tm, d), lambda i, c: (i, 0), pipeline_mode=pl.Buffered(1)),
                  pl.BlockSpec((1, d), lambda i, c: (0, 0)),
                  pl.BlockSpec((d, tf), lambda i, c: (0, c)),
                  pl.BlockSpec((d, tf), lambda i, c: (0, c)),
                  pl.BlockSpec((8, tf), lambda i, c: (0, c)),
                  pl.BlockSpec((1, tf), lambda i, c: (0, c)),
                  pl.BlockSpec((tf, d), lambda i, c: (c, 0)),
                  st_spec],
        out_specs=[pl.BlockSpec((tm, d), lambda i, c: (i, 0)), tail_spec],
        out_shape=[jax.ShapeDtypeStruct((n, d), F32),
                   jax.ShapeDtypeStruct((nseq * per, 8, ffp), F32)],
        scratch_shapes=[pltpu.VMEM((tm, d), BF16),
                        pltpu.VMEM((ffp // tf, 8, tf), F32)],
        compiler_params=_cparams(2),
        name="ffn",
    )(x, g, wa, wb, cw, cb, wo, st)
    return x_new, tail.reshape(nseq, per, 8, ffp)[:, per - 1]


def _norm_kernel(x_ref, g_ref, o_ref):
    o_ref[...] = _rms_rows(x_ref[...]) * g_ref[...]


def _final_norm(x, g):
    n, d = x.shape
    tm = min(ROW_BLOCK, n)
    return pl.pallas_call(
        _norm_kernel,
        grid=(n // tm,),
        in_specs=[pl.BlockSpec((tm, d), lambda i: (i, 0)), pl.BlockSpec((1, d), lambda i: (0, 0))],
        out_specs=pl.BlockSpec((tm, d), lambda i: (i, 0)),
        out_shape=jax.ShapeDtypeStruct((n, d), F32),
        compiler_params=_cparams(1),
        name="final_norm",
    )(x, g)


def _prep_layer(p, dims):
    d_ssm, hd, hf, d_ff, ffp = dims
    wd, wf = hd * LANES, hf * HD_FOX
    o_qd, o_kd, o_vd = d_ssm, d_ssm + wd, d_ssm + 2 * wd
    o_qf, o_kf, o_vf, o_fg = d_ssm + 3 * wd, d_ssm + 3 * wd + wf, d_ssm + 3 * wd + 2 * wf, d_ssm + 3 * wd + 3 * wf
    w_in = p['w_in']
    w_fg = jnp.pad(w_in[:, o_fg:], ((0, 0), (0, LANES - hf)))
    q = dict(p)
    q['w_in_n'] = jnp.concatenate([w_in[:, o_qd:o_kd], w_in[:, o_qf:o_kf], w_fg, w_in[:, o_vd:o_qf]],
                                  axis=1).astype(BF16)
    q['w_in_t'] = jnp.concatenate([w_in[:, :o_qd], w_in[:, o_kd:o_vd], w_in[:, o_kf:o_fg]],
                                  axis=1).T.astype(BF16)
    q['w_glu'] = p['w_glu'].astype(BF16)
    q['w_o'] = p['w_o'].astype(BF16)
    padc = ((0, 0), (0, ffp - d_ff))
    q['w_a'] = jnp.pad(p['w_ffn_in'][:, :d_ff], padc).astype(BF16)
    q['w_b'] = jnp.pad(p['w_ffn_in'][:, d_ff:], padc).astype(BF16)
    q['w_ffn_out'] = jnp.pad(p['w_ffn_out'], ((0, ffp - d_ff), (0, 0))).astype(BF16)
    q['conv_w'] = jnp.pad(p['ffn_conv_w'], ((0, 8 - CONV_W), (0, ffp - d_ff)))
    q['conv_b'] = jnp.pad(p['ffn_conv_b'], (0, ffp - d_ff))[None, :]
    s5 = _s5_params(p['ssm_lam_re'], p['ssm_lam_im'], p['ssm_log_dt'], p['ssm_b_re'], p['ssm_b_im'],
                    p['ssm_c_re'], p['ssm_c_im'], p['ssm_d'], (S5_BLOCK, CHUNK))
    q['s5_seq'], q['s5_short'] = s5[S5_BLOCK], s5[CHUNK]
    q['lam'] = (jnp.exp(jnp.sum(p['diff_lam_q1'] * p['diff_lam_k1']))
                - jnp.exp(jnp.sum(p['diff_lam_q2'] * p['diff_lam_k2'])) + p['lam_init'])
    return q


def _layer(x, bsz, t, li, hist, st, p, dims, bias_tiles, lam_init, slot, nslots, stacks):
    d_ssm, hd, hf, d_ff, ffp = dims
    kdt_h, vd_h, kft_h, vft_h = hist
    lf_c, sr_c, si_c, cv_c = st
    n, d = x.shape
    wd, wf = hd * LANES, hf * HD_FOX
    u, qd, qf, fg, kv = _inproj(x, t, p['g_norm_mix'][None, :], p['w_in_n'], p['w_in_t'],
                                d_ssm, wd, wf, slot, nslots, stacks)
    vd, kdt, kft, vft = kv
    gm = p['g_mix_out'][None, :]
    if t >= min(ROW_BLOCK, n):
        y_ssm, sr, si = _s5_seq(u, sr_c, si_c, p['s5_seq'])
    else:
        y_ssm, sr, si = _s5_short(u[0], sr_c, si_c, p['s5_short'])
    o_ssm = _glu(y_ssm, p['w_glu'], p['b_glu'][None, :], gm[:, :d_ssm])

    o_diff = _diff_attn(qd.reshape(bsz, t, wd), kdt, vd.reshape(nslots, bsz, t * hd, DV_DIFF), slot,
                        kdt_h, vd_h, li, p['lam'], p['rel_bias'], bias_tiles,
                        gm[:, d_ssm:d_ssm + wd], lam_init)

    logf, cum = _logf_cum(fg.reshape(bsz, t, LANES), p['fox_b_f'], lf_c)
    o_fox = _fox_attn(qf.reshape(bsz, t, wf), kft, vft, slot, kft_h, vft_h, li, cum, gm[:, d_ssm + wd:])

    x = _oproj(x, o_ssm, o_diff.reshape(n, wd), o_fox.reshape(n, wf), p['w_o'])

    x, tail = _ffn(x, t, p['g_norm_ffn'][None, :], p['w_a'], p['w_b'], p['conv_w'], p['conv_b'],
                   p['w_ffn_out'], cv_c)
    return x, kv, (logf, sr, si, tail[:, 8 - (CONV_W - 1):, :d_ff])


def kernel(x_prompt, x_sample, cache_diff_k, cache_diff_v, cache_fox_k, cache_fox_v, cache_fox_logf, state_ssm_re, state_ssm_im, state_ffn_conv, g_norm_mix, w_in, ssm_lam_re, ssm_lam_im, ssm_log_dt, ssm_b_re, ssm_b_im, ssm_c_re, ssm_c_im, ssm_d, w_glu, b_glu, diff_lam_q1, diff_lam_k1, diff_lam_q2, diff_lam_k2, rel_bias, fox_b_f, g_mix_out, w_o, g_norm_ffn, w_ffn_in, ffn_conv_w, ffn_conv_b, w_ffn_out, g_final):
    per_layer = {'g_norm_mix': g_norm_mix, 'w_in': w_in, 'ssm_lam_re': ssm_lam_re,
                 'ssm_lam_im': ssm_lam_im, 'ssm_log_dt': ssm_log_dt, 'ssm_b_re': ssm_b_re,
                 'ssm_b_im': ssm_b_im, 'ssm_c_re': ssm_c_re, 'ssm_c_im': ssm_c_im, 'ssm_d': ssm_d,
                 'w_glu': w_glu, 'b_glu': b_glu, 'diff_lam_q1': diff_lam_q1, 'diff_lam_k1': diff_lam_k1,
                 'diff_lam_q2': diff_lam_q2, 'diff_lam_k2': diff_lam_k2, 'fox_b_f': fox_b_f,
                 'g_mix_out': g_mix_out, 'w_o': w_o, 'g_norm_ffn': g_norm_ffn, 'w_ffn_in': w_ffn_in,
                 'ffn_conv_w': ffn_conv_w, 'ffn_conv_b': ffn_conv_b, 'w_ffn_out': w_ffn_out}
    depth = w_in.shape[0]
    bp, tp, d_model = x_prompt.shape
    bs, ts, _ = x_sample.shape
    past = cache_diff_k.shape[2]
    g_ssm = ssm_lam_re.shape[1]
    d_ssm = g_ssm * SSM_GROUP
    hd = cache_diff_k.shape[3]
    hf = cache_fox_k.shape[3]
    wd, wf = hd * LANES, hf * HD_FOX
    d_ff = w_ffn_out.shape[1]
    ffp = -(-d_ff // FF_BLOCK) * FF_BLOCK if d_ff > FF_BLOCK else d_ff
    dims = (d_ssm, hd, hf, d_ff, ffp)

    def conv_state_block(s):
        return jnp.pad(s, ((0, 0), (8 - (CONV_W - 1), 0), (0, ffp - d_ff)))

    hist_sample = (jnp.transpose(cache_diff_k, (0, 1, 3, 4, 5, 2)).reshape(depth, bs, wd, past),
                   cache_diff_v.reshape(depth, bs, past * hd, DV_DIFF),
                   jnp.transpose(cache_fox_k, (0, 1, 3, 4, 2)).reshape(depth, bs, wf, past),
                   jnp.transpose(cache_fox_v, (0, 1, 3, 4, 2)).reshape(depth, bs, wf, past))
    hist_prompt = (None, None, None, None)
    st_prompt = (None, jnp.zeros((bp, g_ssm, P_SSM), F32), jnp.zeros((bp, g_ssm, P_SSM), F32),
                 conv_state_block(jnp.zeros((bp, CONV_W - 1, d_ff), F32)))
    bias_p = _bias_tiles(rel_bias, *_attn_blocks(tp, 0)[:2])
    bias_s = _bias_tiles(rel_bias, *_attn_blocks(ts, past)[:2])

    xp = x_prompt.reshape(bp * tp, d_model)
    xs = x_sample.reshape(bs * ts, d_model)
    kv_p, kv_s, outs_p, outs_s = None, [], [], []
    for li in range(depth):
        lam_init = 0.8 - 0.6 * math.exp(-0.3 * li)
        p = {name: arr[li] for name, arr in per_layer.items()}
        p['rel_bias'] = rel_bias
        p['lam_init'] = lam_init
        p = _prep_layer(p, dims)
        xp, kv_p, sp = _layer(xp, bp, tp, li, hist_prompt, st_prompt, p, dims, bias_p, lam_init,
                              li, depth, kv_p)
        st_sample = (cache_fox_logf[li], state_ssm_re[li], state_ssm_im[li],
                     conv_state_block(state_ffn_conv[li]))
        xs, kv, ss = _layer(xs, bs, ts, li, hist_sample, st_sample, p, dims, bias_s, lam_init,
                            0, 1, None)
        kv_s.append(kv)
        outs_p.append(sp)
        outs_s.append(ss)
    kv_s = [jnp.concatenate([kv[i] for kv in kv_s]) for i in range(4)]

    def assemble(kv, outs, bsz, t):
        vd, kdt, kft, vft = kv
        logf, sr, si, cv = [jnp.stack([s[i] for s in outs]) for i in range(4)]
        return (jnp.transpose(kdt.reshape(depth, bsz, hd, 2, DQK_DIFF, t), (0, 1, 5, 2, 3, 4)),
                vd.reshape(depth, bsz, t, hd, DV_DIFF),
                jnp.transpose(kft.reshape(depth, bsz, hf, HD_FOX, t), (0, 1, 4, 2, 3)),
                jnp.transpose(vft.reshape(depth, bsz, hf, HD_FOX, t), (0, 1, 4, 2, 3)),
                logf, sr, si, cv)

    y_prompt = _final_norm(xp, g_final[None, :]).reshape(bp, tp, d_model)
    y_sample = _final_norm(xs, g_final[None, :]).reshape(bs, ts, d_model)
    return (y_prompt, y_sample, *assemble(kv_p, outs_p, bp, tp), *assemble(kv_s, outs_s, bs, ts))
```

```python
import functools
import math

import numpy as np
import jax
import jax.numpy as jnp
from jax import lax
from jax.experimental import pallas as pl
from jax.experimental.pallas import tpu as pltpu

F32 = jnp.float32
BF16 = jnp.bfloat16

EPS = 1e-6
CHUNK = 64
S5_BLOCK = 128
SSM_GROUP = 16
P_SSM = 64
DQK_DIFF = 64
DV_DIFF = 128
HD_FOX = 64
N_BUCKETS = 32
MAX_DISTANCE = 128
CONV_W = 3
LANES = 128
ATTN_BLOCK = 512
FF_BLOCK = 512
FF_SUB = 256
ROW_BLOCK = 512
FF_ROWS = 1024
V7X_VMEM_LIMIT = 56 * 1024 * 1024
NEG = -1e30
LOG2E = math.log2(math.e)


def _cparams(n_axes):
    return pltpu.CompilerParams(dimension_semantics=("arbitrary",) * n_axes,
                                vmem_limit_bytes=V7X_VMEM_LIMIT)


def _resident(shape):
    nd = len(shape)
    return pl.BlockSpec(shape, lambda *_: (0,) * nd, pipeline_mode=pl.Buffered(1))


def _rms_rows(x):
    return x * lax.rsqrt(jnp.mean(x * x, axis=-1, keepdims=True) + EPS)


def _dot(a, b):
    return jnp.dot(a, b, preferred_element_type=F32)


def _dot_nt(a, b):
    return lax.dot_general(a, b, (((1,), (1,)), ((), ())), preferred_element_type=F32)


def _inproj_kernel(x_ref, g_ref, wn_ref, wt_ref, *refs, d_ssm, wd, wf, ns, u_slabs):
    u_ref, qd_ref, qf_ref, fg_ref, vd_ref, kdt_ref, kft_ref, vft_ref = refs[-8:]
    tm = x_ref.shape[0]
    xn = (_rms_rows(x_ref[...]) * g_ref[...]).astype(BF16)
    c = 0
    qd_ref[...] = (_dot(xn, wn_ref[:, c:c + wd]) * (DQK_DIFF ** -0.5 * LOG2E)).astype(BF16)
    c += wd
    qf_ref[...] = (_dot(xn, wn_ref[:, c:c + wf]) * (HD_FOX ** -0.5 * LOG2E)).astype(BF16)
    c += wf
    fg_ref[...] = _dot(xn, wn_ref[:, c:c + LANES])
    c += LANES
    vd = _dot(xn, wn_ref[:, c:c + wd])
    nh = wd // DV_DIFF
    for h in range(nh):
        vd_ref[0, pl.ds(h, tm, stride=nh), :] = vd[:, h * DV_DIFF:(h + 1) * DV_DIFF]
    ts = tm // ns
    zt = _dot_nt(wt_ref[0:d_ssm, :], xn)
    if u_slabs:
        zg = zt.reshape(d_ssm // SSM_GROUP, SSM_GROUP, tm)
        for k in range(u_slabs):
            u_ref[:, k, 0] = zg[:, :, k * S5_BLOCK:(k + 1) * S5_BLOCK]
    else:
        for s in range(ns):
            u_ref[0, s] = zt[:, s * ts:(s + 1) * ts]
    r = d_ssm
    for o_ref, width in ((kdt_ref, wd), (kft_ref, wf), (vft_ref, wf)):
        zt = _dot_nt(wt_ref[r:r + width, :], xn)
        for s in range(ns):
            o_ref[0, s] = zt[:, s * ts:(s + 1) * ts]
        r += width


def _inproj(x, t, g, wn, wt, d_ssm, wd, wf, slot, nslots, stacks):
    n, d = x.shape
    tm = min(ROW_BLOCK, n)
    nseq = n // t
    nh = wd // DV_DIFF
    assert n % tm == 0 and (t % tm == 0 or tm % t == 0)
    if t >= tm:
        per, ns = t // tm, 1
        tspec = lambda width: pl.BlockSpec((1, 1, width, tm), lambda i: (slot, i // per, 0, i % per))
        u_slabs = tm // S5_BLOCK
        g_ssm = d_ssm // SSM_GROUP
        u_spec = pl.BlockSpec((g_ssm, u_slabs, 1, SSM_GROUP, S5_BLOCK),
                              lambda i: (0, i % per, i // per, 0, 0))
        u_shape = (g_ssm, t // S5_BLOCK, nseq, SSM_GROUP, S5_BLOCK)
    else:
        ns = tm // t
        tspec = lambda width: pl.BlockSpec((1, ns, width, t), lambda i: (slot, i, 0, 0))
        u_slabs = 0
        u_spec = pl.BlockSpec((1, ns, d_ssm, t), lambda i: (0, i, 0, 0))
        u_shape = (1, nseq, d_ssm, t)
    row = lambda width: pl.BlockSpec((tm, width), lambda i: (i, 0))
    in_specs = [row(d), _resident((1, d)), _resident(wn.shape), _resident(wt.shape)]
    args = [x, g, wn, wt]
    aliases = {}
    if stacks is not None:
        in_specs += [pl.BlockSpec(memory_space=pl.ANY)] * len(stacks)
        aliases = {len(args) + k: 4 + k for k in range(len(stacks))}
        args += list(stacks)
    outs = pl.pallas_call(
        functools.partial(_inproj_kernel, d_ssm=d_ssm, wd=wd, wf=wf, ns=ns, u_slabs=u_slabs),
        grid=(n // tm,),
        in_specs=in_specs,
        out_specs=[u_spec, row(wd), row(wf), row(LANES),
                   pl.BlockSpec((1, tm * nh, DV_DIFF), lambda i: (slot, i, 0)),
                   tspec(wd), tspec(wf), tspec(wf)],
        out_shape=[jax.ShapeDtypeStruct(u_shape, F32),
                   jax.ShapeDtypeStruct((n, wd), BF16),
                   jax.ShapeDtypeStruct((n, wf), BF16),
                   jax.ShapeDtypeStruct((n, LANES), F32),
                   jax.ShapeDtypeStruct((nslots, n * nh, DV_DIFF), F32),
                   jax.ShapeDtypeStruct((nslots, nseq, wd, t), F32),
                   jax.ShapeDtypeStruct((nslots, nseq, wf, t), F32),
                   jax.ShapeDtypeStruct((nslots, nseq, wf, t), F32)],
        input_output_aliases=aliases,
        compiler_params=_cparams(1),
        name="inproj",
    )(*args)
    return outs[0], outs[1], outs[2], outs[3], tuple(outs[4:])


def _s5_block_ops(u, kt_ref, p_ref, q_ref, a_ref, d_ref, h0_ref, ht_ref, m_scr, v_scr, hin_scr,
                  nb, bsz, blk, emit):
    width = SSM_GROUP * blk
    lane = lax.broadcasted_iota(jnp.int32, (blk, width), 1)
    row = lax.broadcasted_iota(jnp.int32, (blk, width), 0)
    causal = (lane & (blk - 1)) >= row
    for h in range(SSM_GROUP):
        base = jnp.broadcast_to(kt_ref[0, h:h + 1, :], (blk, width))
        shifted = pltpu.roll(base, 0, 1, stride=1, stride_axis=0)
        m_scr[h * blk:(h + 1) * blk, :] = jnp.where(causal, shifted, 0.0).astype(BF16)

    ub = u.astype(BF16)
    emit(_dot(ub, m_scr[...]) + d_ref[0] * u, None)
    v_scr[...] = _dot(ub, p_ref[0])

    a_same = a_ref[0, 0:1, :]
    a_cross = a_ref[0, 1:2, :]
    h = h0_ref[0]
    for c in range(nb):
        hin_scr[c * bsz:(c + 1) * bsz, :] = h
        h = a_same * h + a_cross * pltpu.roll(h, P_SSM, 1) + v_scr[c * bsz:(c + 1) * bsz, :]
    ht_ref[0] = h
    emit(None, _dot(hin_scr[...].astype(BF16), q_ref[0]))


def _s5_kernel(u_ref, kt_ref, p_ref, q_ref, a_ref, d_ref, h0_ref, y_ref, ht_ref,
               m_scr, v_scr, hin_scr, *, nb, bsz, blk):
    def emit(y_local, y_state):
        if y_local is not None:
            y_ref[0] = y_local
        else:
            y_ref[0] += y_state

    _s5_block_ops(u_ref[0], kt_ref, p_ref, q_ref, a_ref, d_ref, h0_ref, ht_ref, m_scr, v_scr, hin_scr,
                  nb, bsz, blk, emit)


def _s5_seq_kernel(u_ref, kt_ref, p_ref, q_ref, a_ref, d_ref, h0_ref, y_ref, ht_ref,
                   m_scr, v_scr, hin_scr, u_scr, y_scr, *, nb, bsz, blk):
    rows = nb * bsz
    for h in range(SSM_GROUP):
        u_scr[:, h * blk:(h + 1) * blk] = u_ref[0, pl.ds(h, rows, stride=SSM_GROUP), :]

    def emit(y_local, y_state):
        if y_local is not None:
            y_scr[...] = y_local
        else:
            y_scr[...] += y_state

    _s5_block_ops(u_scr[...], kt_ref, p_ref, q_ref, a_ref, d_ref, h0_ref, ht_ref, m_scr, v_scr, hin_scr,
                  nb, bsz, blk, emit)
    for h in range(SSM_GROUP):
        y_ref[0, pl.ds(h, rows, stride=SSM_GROUP), :] = y_scr[:, h * blk:(h + 1) * blk]


def _s5_params(lam_re, lam_im, log_dt, b_re, b_im, c_re, c_im, d_skip, blks):
    hp = lax.Precision.HIGHEST
    g = lam_re.shape[0]
    top = max(blks)
    dt = jnp.exp(log_dt)[:, None]
    k = jnp.arange(0, top + 1, dtype=F32)[:, None, None]
    mag = jnp.exp(lam_re * dt * k)
    pr = mag * jnp.cos(lam_im * dt * k)
    pi = mag * jnp.sin(lam_im * dt * k)
    ar, ai = pr[1], pi[1]
    den = lam_re * lam_re + lam_im * lam_im
    fr = ((ar - 1.0) * lam_re + ai * lam_im) / den
    fi = (ai * lam_re - (ar - 1.0) * lam_im) / den
    bbr = fr[..., None] * b_re - fi[..., None] * b_im
    bbi = fr[..., None] * b_im + fi[..., None] * b_re
    car = c_re[None] * pr[:, :, None, :] - c_im[None] * pi[:, :, None, :]
    cai = c_re[None] * pi[:, :, None, :] + c_im[None] * pr[:, :, None, :]
    kt_top = (jnp.einsum('kgap,gph->ghak', car[:top], bbr, precision=hp)
              - jnp.einsum('kgap,gph->ghak', cai[:top], bbi, precision=hp))
    out = {}
    for blk in blks:
        kt = kt_top[..., :blk].reshape(g, SSM_GROUP, SSM_GROUP * blk)
        prr = pr[blk - 1::-1][:blk]
        pir = pi[blk - 1::-1][:blk]
        p_re = prr[:, :, :, None] * bbr[None] - pir[:, :, :, None] * bbi[None]
        p_im = prr[:, :, :, None] * bbi[None] + pir[:, :, :, None] * bbr[None]
        pmat = jnp.concatenate([jnp.transpose(p_re, (1, 3, 0, 2)), jnp.transpose(p_im, (1, 3, 0, 2))],
                               axis=-1).reshape(g, SSM_GROUP * blk, 2 * P_SSM)
        q_re = jnp.transpose(car[1:blk + 1], (1, 3, 2, 0))
        q_im = -jnp.transpose(cai[1:blk + 1], (1, 3, 2, 0))
        qmat = jnp.concatenate([q_re, q_im], axis=1).reshape(g, 2 * P_SSM, SSM_GROUP * blk)
        a_blk = jnp.stack([jnp.concatenate([pr[blk], pr[blk]], -1),
                           jnp.concatenate([-pi[blk], pi[blk]], -1)], axis=1)
        dvec = jnp.repeat(d_skip, blk, axis=-1)[:, None, :]
        out[blk] = (kt, pmat.astype(BF16), qmat.astype(BF16), a_blk, dvec)
    return out


def _s5_call(kern, u_arr, u_spec, sp, h_re, h_im, nb, bsz, blk, extra_scratch):
    kt, pmat, qmat, a_blk, dvec = sp
    g = kt.shape[0]
    width = SSM_GROUP * blk
    rows = nb * bsz
    h0 = jnp.transpose(jnp.concatenate([h_re, h_im], axis=-1), (1, 0, 2))
    y, ht = pl.pallas_call(
        functools.partial(kern, nb=nb, bsz=bsz, blk=blk),
        grid=(g,),
        in_specs=[u_spec,
                  pl.BlockSpec((1, SSM_GROUP, width), lambda i: (i, 0, 0)),
                  pl.BlockSpec((1, width, 2 * P_SSM), lambda i: (i, 0, 0)),
                  pl.BlockSpec((1, 2 * P_SSM, width), lambda i: (i, 0, 0)),
                  pl.BlockSpec((1, 2, 2 * P_SSM), lambda i: (i, 0, 0)),
                  pl.BlockSpec((1, 1, width), lambda i: (i, 0, 0)),
                  pl.BlockSpec((1, bsz, 2 * P_SSM), lambda i: (i, 0, 0))],
        out_specs=[u_spec, pl.BlockSpec((1, bsz, 2 * P_SSM), lambda i: (i, 0, 0))],
        out_shape=[jax.ShapeDtypeStruct(u_arr.shape, F32),
                   jax.ShapeDtypeStruct((g, bsz, 2 * P_SSM), F32)],
        scratch_shapes=[pltpu.VMEM((width, width), BF16),
                        pltpu.VMEM((rows, 2 * P_SSM), F32),
                        pltpu.VMEM((rows, 2 * P_SSM), F32)] + extra_scratch,
        compiler_params=_cparams(1),
        name="s5",
    )(u_arr, kt, pmat, qmat, a_blk, dvec, h0)
    ht = jnp.transpose(ht, (1, 0, 2))
    return y, ht[..., :P_SSM], ht[..., P_SSM:]


def _s5_short(ut, h_re, h_im, sp):
    bsz, dssm, t = ut.shape
    g = dssm // SSM_GROUP
    nb = t // CHUNK
    width = SSM_GROUP * CHUNK
    rows = nb * bsz
    u = jnp.transpose(ut.reshape(bsz, g, SSM_GROUP, nb, CHUNK), (1, 3, 0, 2, 4)).reshape(g, rows, width)
    y, sr, si = _s5_call(_s5_kernel, u, pl.BlockSpec((1, rows, width), lambda i: (i, 0, 0)),
                         sp, h_re, h_im, nb, bsz, CHUNK, [])
    y = jnp.transpose(y.reshape(g, nb, bsz, SSM_GROUP, CHUNK), (2, 1, 4, 0, 3)).reshape(bsz * t, dssm)
    return y, sr, si


def _s5_seq(u5, h_re, h_im, sp):
    g, nb, bsz, _, blk = u5.shape
    rows, width = nb * bsz, SSM_GROUP * blk
    spec = pl.BlockSpec((1, rows * SSM_GROUP, blk), lambda i: (i, 0, 0))
    y, sr, si = _s5_call(_s5_seq_kernel, u5.reshape(g, rows * SSM_GROUP, blk), spec, sp, h_re, h_im,
                         nb, bsz, blk, [pltpu.VMEM((rows, width), F32), pltpu.VMEM((rows, width), F32)])
    return y.reshape(u5.shape), sr, si


def _glu_math(y, w_ref, b_ref, gm_ref):
    g = 0.5 * y * (1.0 + jnp.tanh(math.sqrt(2.0 / math.pi) * (y + 0.044715 * (y * y * y))))
    z = _dot(g.astype(BF16), w_ref[...]) + b_ref[...]
    yy = g * (1.0 / (1.0 + jnp.exp(-z)))
    return (_rms_rows(yy) * gm_ref[...]).astype(BF16)


def _glu_kernel(y_ref, w_ref, b_ref, gm_ref, o_ref):
    o_ref[...] = _glu_math(y_ref[...], w_ref, b_ref, gm_ref)


def _glu_seq_kernel(y_ref, w_ref, b_ref, gm_ref, o_ref, y_scr):
    g, nk, _, hch, blk = y_ref.shape
    for k in range(nk):
        y_scr[k * blk:(k + 1) * blk, :] = jnp.transpose(y_ref[:, k, 0].reshape(g * hch, blk))
    o_ref[...] = _glu_math(y_scr[...], w_ref, b_ref, gm_ref)


def _glu(y, w, b, gm):
    if y.ndim == 2:
        n, d = y.shape
        tm = min(ROW_BLOCK, n)
        kern, y_spec, scratch = _glu_kernel, pl.BlockSpec((tm, d), lambda i: (i, 0)), []
    else:
        g, nb, bsz, hch, blk = y.shape
        d = g * hch
        n, tm = nb * bsz * blk, ROW_BLOCK
        per = nb * blk // tm
        kern = _glu_seq_kernel
        y_spec = pl.BlockSpec((g, tm // blk, 1, hch, blk), lambda i: (0, i % per, i // per, 0, 0))
        scratch = [pltpu.VMEM((tm, d), F32)]
    return pl.pallas_call(
        kern,
        grid=(n // tm,),
        in_specs=[y_spec, _resident(w.shape), _resident((1, d)), _resident((1, d))],
        out_specs=pl.BlockSpec((tm, d), lambda i: (i, 0)),
        out_shape=jax.ShapeDtypeStruct((n, d), BF16),
        scratch_shapes=scratch,
        compiler_params=_cparams(1),
        name="glu",
    )(y, w, b, gm)


def _t5_bucket_np(rel):
    nb = N_BUCKETS // 2
    max_exact = nb // 2
    n = np.abs(rel)
    large = max_exact + (np.log(np.maximum(n, 1).astype(np.float32) / np.float32(max_exact))
                         / np.float32(math.log(MAX_DISTANCE / max_exact))
                         * np.float32(nb - max_exact)).astype(np.int32)
    large = np.minimum(large, nb - 1)
    return (np.where(rel > 0, nb, 0) + np.where(n < max_exact, n, large)).astype(np.int32)


def _bucket_thresholds():
    dist = np.arange(0, 4 * MAX_DISTANCE, dtype=np.int32)
    bk = _t5_bucket_np(-dist)
    assert np.all(np.diff(bk) >= 0) and bk[-1] == N_BUCKETS // 2 - 1
    assert np.array_equal(_t5_bucket_np(dist[1:]), bk[1:] + N_BUCKETS // 2)
    return [int(np.argmax(bk >= k)) for k in range(N_BUCKETS // 2)]


def _attn_blocks(t, past):
    tq = min(ATTN_BLOCK, t)
    nq = t // tq
    assert t % tq == 0 and tq % CHUNK == 0
    if nq == 1:
        tk = past if past else tq
    else:
        assert past == 0
        tk = tq
    return tq, tk, nq


def _online_update(m, l, acc, s, shift, pv):
    m_new = jnp.maximum(m, jnp.max(s, axis=-1, keepdims=True) + shift)
    alpha = jnp.exp2(m - m_new)
    p = jnp.exp2(s - (m_new - shift))
    if l is not None:
        l = alpha * l + jnp.sum(p, axis=-1, keepdims=True)
    acc = alpha * acc + pv(p.astype(BF16))
    return m_new, l, acc


def _softmax_init(tq, width):
    return (jnp.full((tq, 1), NEG, F32), jnp.zeros((tq, 1), F32), jnp.zeros((tq, width), F32))


def _bias_kernel(rb_ref, near_ref, diag_ref, *, nh, thr):
    h = pl.program_id(0)
    half = N_BUCKETS // 2

    def tile(shape, col_off):
        r = lax.broadcasted_iota(jnp.int32, shape, 0)
        c = lax.broadcasted_iota(jnp.int32, shape, 1)
        rel = c + col_off - r
        dist = jnp.abs(rel)
        back = jnp.full(shape, rb_ref[h], F32)
        fwd = jnp.full(shape, rb_ref[half * nh + h], F32)
        for k in range(1, half):
            far = dist >= thr[k]
            back = jnp.where(far, rb_ref[k * nh + h], back)
            fwd = jnp.where(far, rb_ref[(half + k) * nh + h], fwd)
        return jnp.where(rel > 0, fwd, back) * LOG2E, r, c

    near, _, _ = tile(near_ref.shape[1:], -near_ref.shape[2])
    near_ref[0] = near
    diag, r, c = tile(diag_ref.shape[1:], 0)
    diag_ref[0] = jnp.where((c // CHUNK) <= (r // CHUNK), diag, NEG)


def _bias_tiles(rel_bias, tq, tk):
    nh = rel_bias.shape[1]
    return pl.pallas_call(
        functools.partial(_bias_kernel, nh=nh, thr=_bucket_thresholds()),
        grid=(nh,),
        in_specs=[pl.BlockSpec(memory_space=pltpu.SMEM)],
        out_specs=[pl.BlockSpec((1, tq, tk), lambda h: (h, 0, 0)),
                   pl.BlockSpec((1, tq, tq), lambda h: (h, 0, 0))],
        out_shape=[jax.ShapeDtypeStruct((nh, tq, tk), F32),
                   jax.ShapeDtypeStruct((nh, tq, tq), F32)],
        compiler_params=_cparams(1),
        name="rel_bias_tiles",
    )(rel_bias.reshape(-1))


def _diff_kernel(lam_ref, rb_ref, q_ref, kt_ref, v_ref, *rest, t, past, tq, tk, nh, out_scale):
    if past:
        kth_ref, vh_ref = rest[0], rest[1]
        rest = rest[2:]
    bn_ref, bd_ref, gm_ref, o_ref, kt_scr, v_scr = rest
    h = pl.program_id(1)
    qi = pl.program_id(2)
    nq = t // tq

    @pl.when(qi == 0)
    def _():
        if past:
            kt_scr[:, 0:past] = kth_ref[0, 0].astype(BF16)
            v_scr[0:past, :] = vh_ref[0, 0, pl.ds(h, past, stride=nh), :].astype(BF16)
        kt_scr[:, past:past + t] = kt_ref[0, 0].astype(BF16)
        v_scr[past:past + t, :] = v_ref[0, 0, pl.ds(h, t, stride=nh), :].astype(BF16)

    q = q_ref[0]
    lane = lax.broadcasted_iota(jnp.int32, q.shape, 1)
    q1 = jnp.where(lane < DQK_DIFF, q, jnp.zeros_like(q))
    q2 = jnp.where(lane >= DQK_DIFF, q, jnp.zeros_like(q))
    far_bias = rb_ref[(N_BUCKETS // 2 - 1) * nh + h] * LOG2E

    def block(carry, off, width, bias_ref):
        kt = kt_scr[:, pl.ds(off, width)]
        vb = v_scr[pl.ds(off, width), :]
        c1, c2 = carry
        pv = lambda p: _dot(p, vb)
        s1, s2 = _dot(q1, kt), _dot(q2, kt)
        if bias_ref is None:
            shift = far_bias
        else:
            shift = 0.0
            s1, s2 = s1 + bias_ref[0], s2 + bias_ref[0]
        return _online_update(*c1, s1, shift, pv), _online_update(*c2, s2, shift, pv)

    carry = (_softmax_init(tq, DV_DIFF), _softmax_init(tq, DV_DIFF))
    if nq == 1:
        if past:
            carry = block(carry, 0, past, bn_ref)
        d0 = past
    else:
        carry = lax.fori_loop(
            0, qi - 1, lambda j, cr: block(cr, pl.multiple_of(j * tk, tk), tk, None), carry)
        carry = lax.cond(qi >= 1,
                         lambda cr: block(cr, pl.multiple_of((qi - 1) * tk, tk), tk, bn_ref),
                         lambda cr: cr, carry)
        d0 = pl.multiple_of(qi * tq, tq)
    carry = block(carry, d0, tq, bd_ref)

    (_, l1, a1), (_, l2, a2) = carry
    o = a1 / l1 - lam_ref[0] * (a2 / l2)
    o_ref[0] = (_rms_rows(o) * out_scale * gm_ref[...]).astype(BF16)


def _diff_attn(q, kt, v, slot, kth, vh, li, lam, rel_bias, bias_tiles, gm, lam_init):
    bsz, t, hw = q.shape
    nh = hw // LANES
    past = 0 if kth is None else kth.shape[-1]
    tq, tk, nq = _attn_blocks(t, past)
    bias_near, bias_diag = bias_tiles
    smem = pl.BlockSpec(memory_space=pltpu.SMEM)
    args = [lam.reshape(1), rel_bias.reshape(-1), q, kt, v]
    in_specs = [smem, smem,
                pl.BlockSpec((1, tq, LANES), lambda b, h, i: (b, i, h)),
                pl.BlockSpec((1, 1, LANES, t), lambda b, h, i: (slot, b, h, 0)),
                pl.BlockSpec((1, 1, t * nh, LANES), lambda b, h, i: (slot, b, 0, 0))]
    if past:
        args += [kth, vh]
        in_specs += [pl.BlockSpec((1, 1, LANES, past), lambda b, h, i: (li, b, h, 0)),
                     pl.BlockSpec((1, 1, past * nh, LANES), lambda b, h, i: (li, b, 0, 0))]
    args += [bias_near, bias_diag, gm]
    in_specs += [pl.BlockSpec((1, tq, tk), lambda b, h, i: (h, 0, 0)),
                 pl.BlockSpec((1, tq, tq), lambda b, h, i: (h, 0, 0)),
                 pl.BlockSpec((1, LANES), lambda b, h, i: (0, h))]
    return pl.pallas_call(
        functools.partial(_diff_kernel, t=t, past=past, tq=tq, tk=tk, nh=nh, out_scale=1.0 - lam_init),
        grid=(bsz, nh, nq),
        in_specs=in_specs,
        out_specs=pl.BlockSpec((1, tq, LANES), lambda b, h, i: (b, i, h)),
        out_shape=jax.ShapeDtypeStruct((bsz, t, hw), BF16),
        scratch_shapes=[pltpu.VMEM((LANES, past + t), BF16),
                        pltpu.VMEM((past + t, LANES), BF16)],
        compiler_params=_cparams(3),
        name="diff_attn",
    )(*args)


def _logf_kernel(fg_ref, bf_ref, *rest, t, past, nh):
    if past:
        past_ref, logf_ref, cum_ref = rest
    else:
        logf_ref, cum_ref = rest
    x = fg_ref[0][:, :nh] + bf_ref[...]
    lf = -(jnp.maximum(-x, 0.0) + jnp.log1p(jnp.exp(-jnp.abs(x))))
    logf_ref[0] = lf

    def tri(n):
        r = lax.broadcasted_iota(jnp.int32, (n, n), 0)
        c = lax.broadcasted_iota(jnp.int32, (n, n), 1)
        return (r >= c).astype(F32)

    carry = jnp.zeros((1, nh), F32)
    pos = 0
    for src, length in ((None, past), (lf, t)):
        for r0 in range(0, length, LANES):
            n = min(LANES, length - r0)
            blk = past_ref[0, r0:r0 + n, :] if src is None else src[r0:r0 + n, :]
            cs = jnp.dot(tri(n), blk, preferred_element_type=F32,
                         precision=lax.Precision.HIGHEST) + carry
            cum_ref[0, pos:pos + n, :] = cs
            carry = cs[n - 1:n, :]
            pos += n


def _logf_cum(fg, b_f, past_logf):
    bsz, t, _ = fg.shape
    nh = b_f.shape[-1]
    past = 0 if past_logf is None else past_logf.shape[1]
    args = [fg, b_f.reshape(1, nh)]
    in_specs = [pl.BlockSpec((1, t, LANES), lambda b: (b, 0, 0)),
                pl.BlockSpec((1, nh), lambda b: (0, 0))]
    if past:
        args.append(past_logf)
        in_specs.append(pl.BlockSpec((1, past, nh), lambda b: (b, 0, 0)))
    return pl.pallas_call(
        functools.partial(_logf_kernel, t=t, past=past, nh=nh),
        grid=(bsz,),
        in_specs=in_specs,
        out_specs=[pl.BlockSpec((1, t, nh), lambda b: (b, 0, 0)),
                   pl.BlockSpec((1, past + t, nh), lambda b: (b, 0, 0))],
        out_shape=[jax.ShapeDtypeStruct((bsz, t, nh), F32),
                   jax.ShapeDtypeStruct((bsz, past + t, nh), F32)],
        compiler_params=_cparams(1),
        name="logf_cum",
    )(*args)


def _fox_kernel(q_ref, kt_ref, vt_ref, *rest, t, past, tq, tk):
    if past:
        kth_ref, vth_ref = rest[0], rest[1]
        rest = rest[2:]
    cq_ref, ck_ref, gm_ref, o_ref, kt_scr, vt_scr = rest
    hp = pl.program_id(1)
    qi = pl.program_id(2)
    nq = t // tq

    @pl.when(qi == 0)
    def _():
        vt_scr[...] = jnp.ones(vt_scr.shape, BF16)
        halves = ((0, slice(0, HD_FOX)), (1, slice(HD_FOX, LANES)))
        if past:
            kt_scr[:, 0:past] = kth_ref[0, 0].astype(BF16)
            for hh, own in halves:
                vt_scr[hh, own, 0:past] = vth_ref[0, 0, own, :].astype(BF16)
        kt_scr[:, past:past + t] = kt_ref[0, 0].astype(BF16)
        for hh, own in halves:
            vt_scr[hh, own, past:past + t] = vt_ref[0, 0, own, :].astype(BF16)

    q = q_ref[0]
    lane = lax.broadcasted_iota(jnp.int32, q.shape, 1)
    cq_blk = cq_ref[0]
    head_lane = lax.broadcasted_iota(jnp.int32, cq_blk.shape, 1)
    r = lax.broadcasted_iota(jnp.int32, (tq, tq), 0)
    c = lax.broadcasted_iota(jnp.int32, (tq, tq), 1)
    causal = c <= r

    def head_row(blk, head):
        rid = lax.broadcasted_iota(jnp.int32, blk.shape, 0)
        return jnp.sum(jnp.where(rid == head, blk, 0.0), axis=0, keepdims=True)

    qh, cq = [], []
    for hh in range(2):
        in_half = (lane >= hh * HD_FOX) & (lane < (hh + 1) * HD_FOX)
        qh.append(jnp.where(in_half, q, jnp.zeros_like(q)))
        cq.append(jnp.sum(jnp.where(head_lane == 2 * hp + hh, cq_blk, 0.0), axis=-1, keepdims=True) * LOG2E)

    def block(carry, off, width, mask):
        kt = kt_scr[:, pl.ds(off, width)]
        ck_all = ck_ref[0, :, pl.ds(off, width)] * LOG2E
        out = []
        for hh in range(2):
            s = _dot(qh[hh], kt) - head_row(ck_all, 2 * hp + hh)
            if mask is not None:
                s = jnp.where(mask, s, NEG)
            vt = vt_scr[hh, :, pl.ds(off, width)]
            m, acc = carry[hh]
            m, _, acc = _online_update(m, None, acc, s, cq[hh], lambda p: _dot_nt(p, vt))
            out.append((m, acc))
        return tuple(out)

    carry = tuple(_softmax_init(tq, LANES)[::2] for _ in range(2))
    if nq == 1:
        if past:
            carry = block(carry, 0, past, None)
        d0 = past
    else:
        carry = lax.fori_loop(
            0, qi, lambda j, cr: block(cr, pl.multiple_of(j * tk, tk), tk, None), carry)
        d0 = pl.multiple_of(qi * tq, tq)
    carry = block(carry, d0, tq, causal)

    (_, a0), (_, a1) = carry
    o = jnp.where(lane < HD_FOX, a0 / pltpu.roll(a0, HD_FOX, 1), a1 / pltpu.roll(a1, HD_FOX, 1))
    sq = o * o
    ms0 = jnp.sum(jnp.where(lane < HD_FOX, sq, 0.0), axis=-1, keepdims=True) / HD_FOX
    ms1 = jnp.sum(jnp.where(lane >= HD_FOX, sq, 0.0), axis=-1, keepdims=True) / HD_FOX
    inv = jnp.where(lane < HD_FOX, lax.rsqrt(ms0 + EPS), lax.rsqrt(ms1 + EPS))
    o_ref[0] = (o * inv * gm_ref[...]).astype(BF16)


def _fox_attn(q, kt, vt, slot, kth, vth, li, cum, gm):
    bsz, t, hw = q.shape
    npair = hw // LANES
    past = 0 if kth is None else kth.shape[-1]
    nh = cum.shape[-1]
    tq, tk, nq = _attn_blocks(t, past)
    cq = cum[:, past:, :]
    ck = jnp.swapaxes(cum, 1, 2)
    kv_spec = pl.BlockSpec((1, 1, LANES, t), lambda b, h, i: (slot, b, h, 0))
    args = [q, kt, vt]
    in_specs = [pl.BlockSpec((1, tq, LANES), lambda b, h, i: (b, i, h)), kv_spec, kv_spec]
    if past:
        h_spec = pl.BlockSpec((1, 1, LANES, past), lambda b, h, i: (li, b, h, 0))
        args += [kth, vth]
        in_specs += [h_spec, h_spec]
    args += [cq, ck, gm]
    in_specs += [pl.BlockSpec((1, tq, nh), lambda b, h, i: (b, i, 0)),
                 pl.BlockSpec((1, nh, past + t), lambda b, h, i: (b, 0, 0)),
                 pl.BlockSpec((1, LANES), lambda b, h, i: (0, h))]
    return pl.pallas_call(
        functools.partial(_fox_kernel, t=t, past=past, tq=tq, tk=tk),
        grid=(bsz, npair, nq),
        in_specs=in_specs,
        out_specs=pl.BlockSpec((1, tq, LANES), lambda b, h, i: (b, i, h)),
        out_shape=jax.ShapeDtypeStruct((bsz, t, hw), BF16),
        scratch_shapes=[pltpu.VMEM((LANES, past + t), BF16),
                        pltpu.VMEM((2, LANES, past + t), BF16)],
        compiler_params=_cparams(3),
        name="fox_attn",
    )(*args)


def _oproj_kernel(x_ref, a_ref, b_ref, c_ref, w_ref, o_ref, *, d1, d2):
    acc = _dot(a_ref[...], w_ref[0:d1, :])
    acc += _dot(b_ref[...], w_ref[d1:d2, :])
    acc += _dot(c_ref[...], w_ref[d2:, :])
    o_ref[...] = x_ref[...] + acc


def _oproj(x, a, b, c, w):
    n, d = x.shape
    tm = min(ROW_BLOCK, n)
    d1 = a.shape[1]
    d2 = d1 + b.shape[1]
    row = lambda width: pl.BlockSpec((tm, width), lambda i: (i, 0))
    return pl.pallas_call(
        functools.partial(_oproj_kernel, d1=d1, d2=d2),
        grid=(n // tm,),
        in_specs=[row(d), row(a.shape[1]), row(b.shape[1]), row(c.shape[1]), _resident(w.shape)],
        out_specs=row(d),
        out_shape=jax.ShapeDtypeStruct((n, d), F32),
        compiler_params=_cparams(1),
        name="oproj",
    )(x, a, b, c, w)


def _ffn_kernel(x_ref, g_ref, wa_ref, wb_ref, cw_ref, cb_ref, wo_ref, st_ref, o_ref, tail_ref,
                xn_scr, halo_scr, *, tm, t, tf):
    i = pl.program_id(0)
    c = pl.program_id(1)

    @pl.when(c == 0)
    def _():
        x = x_ref[...]
        xn_scr[...] = (_rms_rows(x) * g_ref[...]).astype(BF16)
        o_ref[...] = x

    tail_rows = 8
    if t >= tm:
        @pl.when((i % (t // tm)) == 0)
        def _():
            halo_scr[c] = st_ref[0]

        halo = halo_scr[c]
    else:
        st = st_ref[...]
        nseq = tm // t

    xn = xn_scr[...]
    rows8 = lax.broadcasted_iota(jnp.int32, (tail_rows, FF_SUB), 0)
    rows = lax.broadcasted_iota(jnp.int32, (tm, FF_SUB), 0)
    gated = []
    for k in range(tf // FF_SUB):
        cols = slice(k * FF_SUB, (k + 1) * FF_SUB)
        a = _dot(xn, wa_ref[:, cols])
        b = _dot(xn, wb_ref[:, cols])
        prev1 = pltpu.roll(a, 1, 0)
        prev2 = pltpu.roll(a, 2, 0)
        if t >= tm:
            h0 = halo[tail_rows - 2:tail_rows - 1, cols]
            h1 = halo[tail_rows - 1:tail_rows, cols]
            head1 = jnp.where(rows8 == 0, h1, prev1[:tail_rows])
            head2 = jnp.where(rows8 == 0, h0, jnp.where(rows8 == 1, h1, prev2[:tail_rows]))
            a1 = jnp.concatenate([head1, prev1[tail_rows:]], axis=0)
            a2 = jnp.concatenate([head2, prev2[tail_rows:]], axis=0)
            last = a[tm - tail_rows:tm, :]
            halo_scr[c, :, cols] = last
            tail_ref[0, :, cols] = last
        else:
            h0 = jnp.broadcast_to(st[:, tail_rows - 2:tail_rows - 1, cols], (nseq, t, FF_SUB)).reshape(tm, FF_SUB)
            h1 = jnp.broadcast_to(st[:, tail_rows - 1:tail_rows, cols], (nseq, t, FF_SUB)).reshape(tm, FF_SUB)
            rmod = rows & (t - 1)
            a1 = jnp.where(rmod == 0, h1, prev1)
            a2 = jnp.where(rmod == 0, h0, jnp.where(rmod == 1, h1, prev2))
            tail_ref[:, :, cols] = a.reshape(nseq, t, FF_SUB)[:, t - tail_rows:, :]
        ac = cw_ref[2:3, cols] * a + cw_ref[1:2, cols] * a1 + cw_ref[0:1, cols] * a2 + cb_ref[:, cols]
        gated.append((ac * (1.0 / (1.0 + jnp.exp(-ac))) * b).astype(BF16))
    o_ref[...] += _dot(jnp.concatenate(gated, axis=1), wo_ref[...])


def _ffn(x, t, g, wa, wb, cw, cb, wo, st):
    n, d = x.shape
    ffp = wo.shape[0]
    tf = min(FF_BLOCK, ffp)
    tm = min(FF_ROWS, n)
    assert ffp % tf == 0 and tf % FF_SUB == 0 and n % tm == 0 and (t % tm == 0 or tm % t == 0)
    assert t & (t - 1) == 0
    nseq = n // t
    if t >= tm:
        per = t // tm
        st_spec = pl.BlockSpec((1, 8, tf), lambda i, c: (i // per, 0, c))
        tail_spec = pl.BlockSpec((1, 8, tf), lambda i, c: (i, 0, c))
    else:
        per = 1
        st_spec = tail_spec = pl.BlockSpec((tm // t, 8, tf), lambda i, c: (i, 0, c))
    x_new, tail = pl.pallas_call(
        functools.partial(_ffn_kernel, tm=tm, t=t, tf=tf),
        grid=(n // tm, ffp // tf),
        in_specs=[pl.BlockSpec((tm, d), lambda i, c: (i, 0)),
                  pl.BlockSpec((1, d), lambda i, c: (0, 0)),
                  pl.BlockSpec((d, tf), lambda i, c: (0, c)),
                  pl.BlockSpec((d, tf), lambda i, c: (0, c)),
                  pl.BlockSpec((8, tf), lambda i, c: (0, c)),
                  pl.BlockSpec((1, tf), lambda i, c: (0, c)),
                  pl.BlockSpec((tf, d), lambda i, c: (c, 0)),
                  st_spec],
        out_specs=[pl.BlockSpec((tm, d), lambda i, c: (i, 0)), tail_spec],
        out_shape=[jax.ShapeDtypeStruct((n, d), F32),
                   jax.ShapeDtypeStruct((nseq * per, 8, ffp), F32)],
        scratch_shapes=[pltpu.VMEM((tm, d), BF16),
                        pltpu.VMEM((ffp // tf, 8, tf), F32)],
        compiler_params=_cparams(2),
        name="ffn",
    )(x, g, wa, wb, cw, cb, wo, st)
    return x_new, tail.reshape(nseq, per, 8, ffp)[:, per - 1]


def _norm_kernel(x_ref, g_ref, o_ref):
    o_ref[...] = _rms_rows(x_ref[...]) * g_ref[...]


def _final_norm(x, g):
    n, d = x.shape
    tm = min(ROW_BLOCK, n)
    return pl.pallas_call(
        _norm_kernel,
        grid=(n // tm,),
        in_specs=[pl.BlockSpec((tm, d), lambda i: (i, 0)), pl.BlockSpec((1, d), lambda i: (0, 0))],
        out_specs=pl.BlockSpec((tm, d), lambda i: (i, 0)),
        out_shape=jax.ShapeDtypeStruct((n, d), F32),
        compiler_params=_cparams(1),
        name="final_norm",
    )(x, g)


def _prep_layer(p, dims):
    d_ssm, hd, hf, d_ff, ffp = dims
    wd, wf = hd * LANES, hf * HD_FOX
    o_qd, o_kd, o_vd = d_ssm, d_ssm + wd, d_ssm + 2 * wd
    o_qf, o_kf, o_vf, o_fg = d_ssm + 3 * wd, d_ssm + 3 * wd + wf, d_ssm + 3 * wd + 2 * wf, d_ssm + 3 * wd + 3 * wf
    w_in = p['w_in']
    w_fg = jnp.pad(w_in[:, o_fg:], ((0, 0), (0, LANES - hf)))
    q = dict(p)
    q['w_in_n'] = jnp.concatenate([w_in[:, o_qd:o_kd], w_in[:, o_qf:o_kf], w_fg, w_in[:, o_vd:o_qf]],
                                  axis=1).astype(BF16)
    q['w_in_t'] = jnp.concatenate([w_in[:, :o_qd], w_in[:, o_kd:o_vd], w_in[:, o_kf:o_fg]],
                                  axis=1).T.astype(BF16)
    q['w_glu'] = p['w_glu'].astype(BF16)
    q['w_o'] = p['w_o'].astype(BF16)
    padc = ((0, 0), (0, ffp - d_ff))
    q['w_a'] = jnp.pad(p['w_ffn_in'][:, :d_ff], padc).astype(BF16)
    q['w_b'] = jnp.pad(p['w_ffn_in'][:, d_ff:], padc).astype(BF16)
    q['w_ffn_out'] = jnp.pad(p['w_ffn_out'], ((0, ffp - d_ff), (0, 0))).astype(BF16)
    q['conv_w'] = jnp.pad(p['ffn_conv_w'], ((0, 8 - CONV_W), (0, ffp - d_ff)))
    q['conv_b'] = jnp.pad(p['ffn_conv_b'], (0, ffp - d_ff))[None, :]
    s5 = _s5_params(p['ssm_lam_re'], p['ssm_lam_im'], p['ssm_log_dt'], p['ssm_b_re'], p['ssm_b_im'],
                    p['ssm_c_re'], p['ssm_c_im'], p['ssm_d'], (S5_BLOCK, CHUNK))
    q['s5_seq'], q['s5_short'] = s5[S5_BLOCK], s5[CHUNK]
    q['lam'] = (jnp.exp(jnp.sum(p['diff_lam_q1'] * p['diff_lam_k1']))
                - jnp.exp(jnp.sum(p['diff_lam_q2'] * p['diff_lam_k2'])) + p['lam_init'])
    return q


def _layer(x, bsz, t, li, hist, st, p, dims, bias_tiles, lam_init, slot, nslots, stacks):
    d_ssm, hd, hf, d_ff, ffp = dims
    kdt_h, vd_h, kft_h, vft_h = hist
    lf_c, sr_c, si_c, cv_c = st
    n, d = x.shape
    wd, wf = hd * LANES, hf * HD_FOX
    u, qd, qf, fg, kv = _inproj(x, t, p['g_norm_mix'][None, :], p['w_in_n'], p['w_in_t'],
                                d_ssm, wd, wf, slot, nslots, stacks)
    vd, kdt, kft, vft = kv
    gm = p['g_mix_out'][None, :]
    if t >= min(ROW_BLOCK, n):
        y_ssm, sr, si = _s5_seq(u, sr_c, si_c, p['s5_seq'])
    else:
        y_ssm, sr, si = _s5_short(u[0], sr_c, si_c, p['s5_short'])
    o_ssm = _glu(y_ssm, p['w_glu'], p['b_glu'][None, :], gm[:, :d_ssm])

    o_diff = _diff_attn(qd.reshape(bsz, t, wd), kdt, vd.reshape(nslots, bsz, t * hd, DV_DIFF), slot,
                        kdt_h, vd_h, li, p['lam'], p['rel_bias'], bias_tiles,
                        gm[:, d_ssm:d_ssm + wd], lam_init)

    logf, cum = _logf_cum(fg.reshape(bsz, t, LANES), p['fox_b_f'], lf_c)
    o_fox = _fox_attn(qf.reshape(bsz, t, wf), kft, vft, slot, kft_h, vft_h, li, cum, gm[:, d_ssm + wd:])

    x = _oproj(x, o_ssm, o_diff.reshape(n, wd), o_fox.reshape(n, wf), p['w_o'])

    x, tail = _ffn(x, t, p['g_norm_ffn'][None, :], p['w_a'], p['w_b'], p['conv_w'], p['conv_b'],
                   p['w_ffn_out'], cv_c)
    return x, kv, (logf, sr, si, tail[:, 8 - (CONV_W - 1):, :d_ff])


def kernel(x_prompt, x_sample, cache_diff_k, cache_diff_v, cache_fox_k, cache_fox_v, cache_fox_logf, state_ssm_re, state_ssm_im, state_ffn_conv, g_norm_mix, w_in, ssm_lam_re, ssm_lam_im, ssm_log_dt, ssm_b_re, ssm_b_im, ssm_c_re, ssm_c_im, ssm_d, w_glu, b_glu, diff_lam_q1, diff_lam_k1, diff_lam_q2, diff_lam_k2, rel_bias, fox_b_f, g_mix_out, w_o, g_norm_ffn, w_ffn_in, ffn_conv_w, ffn_conv_b, w_ffn_out, g_final):
    per_layer = {'g_norm_mix': g_norm_mix, 'w_in': w_in, 'ssm_lam_re': ssm_lam_re,
                 'ssm_lam_im': ssm_lam_im, 'ssm_log_dt': ssm_log_dt, 'ssm_b_re': ssm_b_re,
                 'ssm_b_im': ssm_b_im, 'ssm_c_re': ssm_c_re, 'ssm_c_im': ssm_c_im, 'ssm_d': ssm_d,
                 'w_glu': w_glu, 'b_glu': b_glu, 'diff_lam_q1': diff_lam_q1, 'diff_lam_k1': diff_lam_k1,
                 'diff_lam_q2': diff_lam_q2, 'diff_lam_k2': diff_lam_k2, 'fox_b_f': fox_b_f,
                 'g_mix_out': g_mix_out, 'w_o': w_o, 'g_norm_ffn': g_norm_ffn, 'w_ffn_in': w_ffn_in,
                 'ffn_conv_w': ffn_conv_w, 'ffn_conv_b': ffn_conv_b, 'w_ffn_out': w_ffn_out}
    depth = w_in.shape[0]
    bp, tp, d_model = x_prompt.shape
    bs, ts, _ = x_sample.shape
    past = cache_diff_k.shape[2]
    g_ssm = ssm_lam_re.shape[1]
    d_ssm = g_ssm * SSM_GROUP
    hd = cache_diff_k.shape[3]
    hf = cache_fox_k.shape[3]
    wd, wf = hd * LANES, hf * HD_FOX
    d_ff = w_ffn_out.shape[1]
    ffp = -(-d_ff // FF_BLOCK) * FF_BLOCK if d_ff > FF_BLOCK else d_ff
    dims = (d_ssm, hd, hf, d_ff, ffp)

    def conv_state_block(s):
        return jnp.pad(s, ((0, 0), (8 - (CONV_W - 1), 0), (0, ffp - d_ff)))

    hist_sample = (jnp.transpose(cache_diff_k, (0, 1, 3, 4, 5, 2)).reshape(depth, bs, wd, past),
                   cache_diff_v.reshape(depth, bs, past * hd, DV_DIFF),
                   jnp.transpose(cache_fox_k, (0, 1, 3, 4, 2)).reshape(depth, bs, wf, past),
                   jnp.transpose(cache_fox_v, (0, 1, 3, 4, 2)).reshape(depth, bs, wf, past))
    hist_prompt = (None, None, None, None)
    st_prompt = (None, jnp.zeros((bp, g_ssm, P_SSM), F32), jnp.zeros((bp, g_ssm, P_SSM), F32),
                 conv_state_block(jnp.zeros((bp, CONV_W - 1, d_ff), F32)))
    bias_p = _bias_tiles(rel_bias, *_attn_blocks(tp, 0)[:2])
    bias_s = _bias_tiles(rel_bias, *_attn_blocks(ts, past)[:2])

    xp = x_prompt.reshape(bp * tp, d_model)
    xs = x_sample.reshape(bs * ts, d_model)
    kv_p, kv_s, outs_p, outs_s = None, [], [], []
    for li in range(depth):
        lam_init = 0.8 - 0.6 * math.exp(-0.3 * li)
        p = {name: arr[li] for name, arr in per_layer.items()}
        p['rel_bias'] = rel_bias
        p['lam_init'] = lam_init
        p = _prep_layer(p, dims)
        xp, kv_p, sp = _layer(xp, bp, tp, li, hist_prompt, st_prompt, p, dims, bias_p, lam_init,
                              li, depth, kv_p)
        st_sample = (cache_fox_logf[li], state_ssm_re[li], state_ssm_im[li],
                     conv_state_block(state_ffn_conv[li]))
        xs, kv, ss = _layer(xs, bs, ts, li, hist_sample, st_sample, p, dims, bias_s, lam_init,
                            0, 1, None)
        kv_s.append(kv)
        outs_p.append(sp)
        outs_s.append(ss)
    kv_s = [jnp.concatenate([kv[i] for kv in kv_s]) for i in range(4)]

    def assemble(kv, outs, bsz, t):
        vd, kdt, kft, vft = kv
        logf, sr, si, cv = [jnp.stack([s[i] for s in outs]) for i in range(4)]
        return (jnp.transpose(kdt.reshape(depth, bsz, hd, 2, DQK_DIFF, t), (0, 1, 5, 2, 3, 4)),
                vd.reshape(depth, bsz, t, hd, DV_DIFF),
                jnp.transpose(kft.reshape(depth, bsz, hf, HD_FOX, t), (0, 1, 4, 2, 3)),
                jnp.transpose(vft.reshape(depth, bsz, hf, HD_FOX, t), (0, 1, 4, 2, 3)),
                logf, sr, si, cv)

    y_prompt = _final_norm(xp, g_final[None, :]).reshape(bp, tp, d_model)
    y_sample = _final_norm(xs, g_final[None, :]).reshape(bs, ts, d_model)
    return (y_prompt, y_sample, *assemble(kv_p, outs_p, bp, tp), *assemble(kv_s, outs_s, bs, ts))
```

```python
import functools
import math

import numpy as np
import jax
import jax.numpy as jnp
from jax import lax
from jax.experimental import pallas as pl
from jax.experimental.pallas import tpu as pltpu

F32 = jnp.float32
BF16 = jnp.bfloat16

EPS = 1e-6
CHUNK = 64
S5_BLOCK = 128
SSM_GROUP = 16
P_SSM = 64
DQK_DIFF = 64
DV_DIFF = 128
HD_FOX = 64
N_BUCKETS = 32
MAX_DISTANCE = 128
CONV_W = 3
LANES = 128
ATTN_BLOCK = 1024
FF_BLOCK = 512
FF_SUB = 256
ROW_BLOCK = 512
FF_ROWS = 1024
V7X_VMEM_LIMIT = 56 * 1024 * 1024
NEG = -1e30
LOG2E = math.log2(math.e)


def _cparams(n_axes):
    return pltpu.CompilerParams(dimension_semantics=("arbitrary",) * n_axes,
                                vmem_limit_bytes=V7X_VMEM_LIMIT)


def _resident(shape):
    nd = len(shape)
    return pl.BlockSpec(shape, lambda *_: (0,) * nd, pipeline_mode=pl.Buffered(1))


def _rms_rows(x):
    return x * lax.rsqrt(jnp.mean(x * x, axis=-1, keepdims=True) + EPS)


def _dot(a, b):
    return jnp.dot(a, b, preferred_element_type=F32)


def _dot_nt(a, b):
    return lax.dot_general(a, b, (((1,), (1,)), ((), ())), preferred_element_type=F32)


def _inproj_kernel(x_ref, g_ref, wn_ref, wt_ref, *refs, d_ssm, wd, wf, ns, u_slabs):
    u_ref, qd_ref, qf_ref, fg_ref, vd_ref, kdt_ref, kft_ref, vft_ref = refs[-8:]
    tm = x_ref.shape[0]
    xn = (_rms_rows(x_ref[...]) * g_ref[...]).astype(BF16)
    c = 0
    qd_ref[...] = (_dot(xn, wn_ref[:, c:c + wd]) * (DQK_DIFF ** -0.5 * LOG2E)).astype(BF16)
    c += wd
    qf_ref[...] = (_dot(xn, wn_ref[:, c:c + wf]) * (HD_FOX ** -0.5 * LOG2E)).astype(BF16)
    c += wf
    fg_ref[...] = _dot(xn, wn_ref[:, c:c + LANES])
    c += LANES
    vd = _dot(xn, wn_ref[:, c:c + wd])
    nh = wd // DV_DIFF
    for h in range(nh):
        vd_ref[0, pl.ds(h, tm, stride=nh), :] = vd[:, h * DV_DIFF:(h + 1) * DV_DIFF]
    ts = tm // ns
    zt = _dot_nt(wt_ref[0:d_ssm, :], xn)
    if u_slabs:
        zg = zt.reshape(d_ssm // SSM_GROUP, SSM_GROUP, tm)
        for k in range(u_slabs):
            u_ref[:, k, 0] = zg[:, :, k * S5_BLOCK:(k + 1) * S5_BLOCK]
    else:
        for s in range(ns):
            u_ref[0, s] = zt[:, s * ts:(s + 1) * ts]
    r = d_ssm
    for o_ref, width in ((kdt_ref, wd), (kft_ref, wf), (vft_ref, wf)):
        zt = _dot_nt(wt_ref[r:r + width, :], xn)
        for s in range(ns):
            o_ref[0, s] = zt[:, s * ts:(s + 1) * ts]
        r += width


def _inproj(x, t, g, wn, wt, d_ssm, wd, wf, slot, nslots, stacks):
    n, d = x.shape
    tm = min(ROW_BLOCK, n)
    nseq = n // t
    nh = wd // DV_DIFF
    assert n % tm == 0 and (t % tm == 0 or tm % t == 0)
    if t >= tm:
        per, ns = t // tm, 1
        tspec = lambda width: pl.BlockSpec((1, 1, width, tm), lambda i: (slot, i // per, 0, i % per))
        u_slabs = tm // S5_BLOCK
        g_ssm = d_ssm // SSM_GROUP
        u_spec = pl.BlockSpec((g_ssm, u_slabs, 1, SSM_GROUP, S5_BLOCK),
                              lambda i: (0, i % per, i // per, 0, 0))
        u_shape = (g_ssm, t // S5_BLOCK, nseq, SSM_GROUP, S5_BLOCK)
    else:
        ns = tm // t
        tspec = lambda width: pl.BlockSpec((1, ns, width, t), lambda i: (slot, i, 0, 0))
        u_slabs = 0
        u_spec = pl.BlockSpec((1, ns, d_ssm, t), lambda i: (0, i, 0, 0))
        u_shape = (1, nseq, d_ssm, t)
    row = lambda width: pl.BlockSpec((tm, width), lambda i: (i, 0))
    in_specs = [row(d), _resident((1, d)), _resident(wn.shape), _resident(wt.shape)]
    args = [x, g, wn, wt]
    aliases = {}
    if stacks is not None:
        in_specs += [pl.BlockSpec(memory_space=pl.ANY)] * len(stacks)
        aliases = {len(args) + k: 4 + k for k in range(len(stacks))}
        args += list(stacks)
    outs = pl.pallas_call(
        functools.partial(_inproj_kernel, d_ssm=d_ssm, wd=wd, wf=wf, ns=ns, u_slabs=u_slabs),
        grid=(n // tm,),
        in_specs=in_specs,
        out_specs=[u_spec, row(wd), row(wf), row(LANES),
                   pl.BlockSpec((1, tm * nh, DV_DIFF), lambda i: (slot, i, 0)),
                   tspec(wd), tspec(wf), tspec(wf)],
        out_shape=[jax.ShapeDtypeStruct(u_shape, F32),
                   jax.ShapeDtypeStruct((n, wd), BF16),
                   jax.ShapeDtypeStruct((n, wf), BF16),
                   jax.ShapeDtypeStruct((n, LANES), F32),
                   jax.ShapeDtypeStruct((nslots, n * nh, DV_DIFF), F32),
                   jax.ShapeDtypeStruct((nslots, nseq, wd, t), F32),
                   jax.ShapeDtypeStruct((nslots, nseq, wf, t), F32),
                   jax.ShapeDtypeStruct((nslots, nseq, wf, t), F32)],
        input_output_aliases=aliases,
        compiler_params=_cparams(1),
        name="inproj",
    )(*args)
    return outs[0], outs[1], outs[2], outs[3], tuple(outs[4:])


def _s5_block_ops(u, kt_ref, p_ref, q_ref, a_ref, d_ref, h0_ref, ht_ref, m_scr, v_scr, hin_scr,
                  nb, bsz, blk, emit):
    width = SSM_GROUP * blk
    lane = lax.broadcasted_iota(jnp.int32, (blk, width), 1)
    row = lax.broadcasted_iota(jnp.int32, (blk, width), 0)
    causal = (lane & (blk - 1)) >= row
    for h in range(SSM_GROUP):
        base = jnp.broadcast_to(kt_ref[0, h:h + 1, :], (blk, width))
        shifted = pltpu.roll(base, 0, 1, stride=1, stride_axis=0)
        m_scr[h * blk:(h + 1) * blk, :] = jnp.where(causal, shifted, 0.0).astype(BF16)

    ub = u.astype(BF16)
    emit(_dot(ub, m_scr[...]) + d_ref[0] * u, None)
    v_scr[...] = _dot(ub, p_ref[0])

    a_same = a_ref[0, 0:1, :]
    a_cross = a_ref[0, 1:2, :]
    h = h0_ref[0]
    for c in range(nb):
        hin_scr[c * bsz:(c + 1) * bsz, :] = h
        h = a_same * h + a_cross * pltpu.roll(h, P_SSM, 1) + v_scr[c * bsz:(c + 1) * bsz, :]
    ht_ref[0] = h
    emit(None, _dot(hin_scr[...].astype(BF16), q_ref[0]))


def _s5_kernel(u_ref, kt_ref, p_ref, q_ref, a_ref, d_ref, h0_ref, y_ref, ht_ref,
               m_scr, v_scr, hin_scr, *, nb, bsz, blk):
    def emit(y_local, y_state):
        if y_local is not None:
            y_ref[0] = y_local
        else:
            y_ref[0] += y_state

    _s5_block_ops(u_ref[0], kt_ref, p_ref, q_ref, a_ref, d_ref, h0_ref, ht_ref, m_scr, v_scr, hin_scr,
                  nb, bsz, blk, emit)


def _s5_seq_kernel(u_ref, kt_ref, p_ref, q_ref, a_ref, d_ref, h0_ref, y_ref, ht_ref,
                   m_scr, v_scr, hin_scr, u_scr, y_scr, *, nb, bsz, blk):
    rows = nb * bsz
    for h in range(SSM_GROUP):
        u_scr[:, h * blk:(h + 1) * blk] = u_ref[0, pl.ds(h, rows, stride=SSM_GROUP), :]

    def emit(y_local, y_state):
        if y_local is not None:
            y_scr[...] = y_local
        else:
            y_scr[...] += y_state

    _s5_block_ops(u_scr[...], kt_ref, p_ref, q_ref, a_ref, d_ref, h0_ref, ht_ref, m_scr, v_scr, hin_scr,
                  nb, bsz, blk, emit)
    for h in range(SSM_GROUP):
        y_ref[0, pl.ds(h, rows, stride=SSM_GROUP), :] = y_scr[:, h * blk:(h + 1) * blk]


def _s5_params(lam_re, lam_im, log_dt, b_re, b_im, c_re, c_im, d_skip, blks):
    hp = lax.Precision.HIGHEST
    g = lam_re.shape[0]
    top = max(blks)
    dt = jnp.exp(log_dt)[:, None]
    k = jnp.arange(0, top + 1, dtype=F32)[:, None, None]
    mag = jnp.exp(lam_re * dt * k)
    pr = mag * jnp.cos(lam_im * dt * k)
    pi = mag * jnp.sin(lam_im * dt * k)
    ar, ai = pr[1], pi[1]
    den = lam_re * lam_re + lam_im * lam_im
    fr = ((ar - 1.0) * lam_re + ai * lam_im) / den
    fi = (ai * lam_re - (ar - 1.0) * lam_im) / den
    bbr = fr[..., None] * b_re - fi[..., None] * b_im
    bbi = fr[..., None] * b_im + fi[..., None] * b_re
    car = c_re[None] * pr[:, :, None, :] - c_im[None] * pi[:, :, None, :]
    cai = c_re[None] * pi[:, :, None, :] + c_im[None] * pr[:, :, None, :]
    kt_top = (jnp.einsum('kgap,gph->ghak', car[:top], bbr, precision=hp)
              - jnp.einsum('kgap,gph->ghak', cai[:top], bbi, precision=hp))
    out = {}
    for blk in blks:
        kt = kt_top[..., :blk].reshape(g, SSM_GROUP, SSM_GROUP * blk)
        prr = pr[blk - 1::-1][:blk]
        pir = pi[blk - 1::-1][:blk]
        p_re = prr[:, :, :, None] * bbr[None] - pir[:, :, :, None] * bbi[None]
        p_im = prr[:, :, :, None] * bbi[None] + pir[:, :, :, None] * bbr[None]
        pmat = jnp.concatenate([jnp.transpose(p_re, (1, 3, 0, 2)), jnp.transpose(p_im, (1, 3, 0, 2))],
                               axis=-1).reshape(g, SSM_GROUP * blk, 2 * P_SSM)
        q_re = jnp.transpose(car[1:blk + 1], (1, 3, 2, 0))
        q_im = -jnp.transpose(cai[1:blk + 1], (1, 3, 2, 0))
        qmat = jnp.concatenate([q_re, q_im], axis=1).reshape(g, 2 * P_SSM, SSM_GROUP * blk)
        a_blk = jnp.stack([jnp.concatenate([pr[blk], pr[blk]], -1),
                           jnp.concatenate([-pi[blk], pi[blk]], -1)], axis=1)
        dvec = jnp.repeat(d_skip, blk, axis=-1)[:, None, :]
        out[blk] = (kt, pmat.astype(BF16), qmat.astype(BF16), a_blk, dvec)
    return out


def _s5_call(kern, u_arr, u_spec, sp, h_re, h_im, nb, bsz, blk, extra_scratch):
    kt, pmat, qmat, a_blk, dvec = sp
    g = kt.shape[0]
    width = SSM_GROUP * blk
    rows = nb * bsz
    h0 = jnp.transpose(jnp.concatenate([h_re, h_im], axis=-1), (1, 0, 2))
    y, ht = pl.pallas_call(
        functools.partial(kern, nb=nb, bsz=bsz, blk=blk),
        grid=(g,),
        in_specs=[u_spec,
                  pl.BlockSpec((1, SSM_GROUP, width), lambda i: (i, 0, 0)),
                  pl.BlockSpec((1, width, 2 * P_SSM), lambda i: (i, 0, 0)),
                  pl.BlockSpec((1, 2 * P_SSM, width), lambda i: (i, 0, 0)),
                  pl.BlockSpec((1, 2, 2 * P_SSM), lambda i: (i, 0, 0)),
                  pl.BlockSpec((1, 1, width), lambda i: (i, 0, 0)),
                  pl.BlockSpec((1, bsz, 2 * P_SSM), lambda i: (i, 0, 0))],
        out_specs=[u_spec, pl.BlockSpec((1, bsz, 2 * P_SSM), lambda i: (i, 0, 0))],
        out_shape=[jax.ShapeDtypeStruct(u_arr.shape, F32),
                   jax.ShapeDtypeStruct((g, bsz, 2 * P_SSM), F32)],
        scratch_shapes=[pltpu.VMEM((width, width), BF16),
                        pltpu.VMEM((rows, 2 * P_SSM), F32),
                        pltpu.VMEM((rows, 2 * P_SSM), F32)] + extra_scratch,
        compiler_params=_cparams(1),
        name="s5",
    )(u_arr, kt, pmat, qmat, a_blk, dvec, h0)
    ht = jnp.transpose(ht, (1, 0, 2))
    return y, ht[..., :P_SSM], ht[..., P_SSM:]


def _s5_short(ut, h_re, h_im, sp):
    bsz, dssm, t = ut.shape
    g = dssm // SSM_GROUP
    nb = t // CHUNK
    width = SSM_GROUP * CHUNK
    rows = nb * bsz
    u = jnp.transpose(ut.reshape(bsz, g, SSM_GROUP, nb, CHUNK), (1, 3, 0, 2, 4)).reshape(g, rows, width)
    y, sr, si = _s5_call(_s5_kernel, u, pl.BlockSpec((1, rows, width), lambda i: (i, 0, 0)),
                         sp, h_re, h_im, nb, bsz, CHUNK, [])
    y = jnp.transpose(y.reshape(g, nb, bsz, SSM_GROUP, CHUNK), (2, 1, 4, 0, 3)).reshape(bsz * t, dssm)
    return y, sr, si


def _s5_seq(u5, h_re, h_im, sp):
    g, nb, bsz, _, blk = u5.shape
    rows, width = nb * bsz, SSM_GROUP * blk
    spec = pl.BlockSpec((1, rows * SSM_GROUP, blk), lambda i: (i, 0, 0))
    y, sr, si = _s5_call(_s5_seq_kernel, u5.reshape(g, rows * SSM_GROUP, blk), spec, sp, h_re, h_im,
                         nb, bsz, blk, [pltpu.VMEM((rows, width), F32), pltpu.VMEM((rows, width), F32)])
    return y.reshape(u5.shape), sr, si


def _glu_math(y, w_ref, b_ref, gm_ref):
    g = 0.5 * y * (1.0 + jnp.tanh(math.sqrt(2.0 / math.pi) * (y + 0.044715 * (y * y * y))))
    z = _dot(g.astype(BF16), w_ref[...]) + b_ref[...]
    yy = g * (1.0 / (1.0 + jnp.exp(-z)))
    return (_rms_rows(yy) * gm_ref[...]).astype(BF16)


def _glu_kernel(y_ref, w_ref, b_ref, gm_ref, o_ref):
    o_ref[...] = _glu_math(y_ref[...], w_ref, b_ref, gm_ref)


def _glu_seq_kernel(y_ref, w_ref, b_ref, gm_ref, o_ref, y_scr):
    g, nk, _, hch, blk = y_ref.shape
    for k in range(nk):
        y_scr[k * blk:(k + 1) * blk, :] = jnp.transpose(y_ref[:, k, 0].reshape(g * hch, blk))
    o_ref[...] = _glu_math(y_scr[...], w_ref, b_ref, gm_ref)


def _glu(y, w, b, gm):
    if y.ndim == 2:
        n, d = y.shape
        tm = min(ROW_BLOCK, n)
        kern, y_spec, scratch = _glu_kernel, pl.BlockSpec((tm, d), lambda i: (i, 0)), []
    else:
        g, nb, bsz, hch, blk = y.shape
        d = g * hch
        n, tm = nb * bsz * blk, ROW_BLOCK
        per = nb * blk // tm
        kern = _glu_seq_kernel
        y_spec = pl.BlockSpec((g, tm // blk, 1, hch, blk), lambda i: (0, i % per, i // per, 0, 0))
        scratch = [pltpu.VMEM((tm, d), F32)]
    return pl.pallas_call(
        kern,
        grid=(n // tm,),
        in_specs=[y_spec, _resident(w.shape), _resident((1, d)), _resident((1, d))],
        out_specs=pl.BlockSpec((tm, d), lambda i: (i, 0)),
        out_shape=jax.ShapeDtypeStruct((n, d), BF16),
        scratch_shapes=scratch,
        compiler_params=_cparams(1),
        name="glu",
    )(y, w, b, gm)


def _t5_bucket_np(rel):
    nb = N_BUCKETS // 2
    max_exact = nb // 2
    n = np.abs(rel)
    large = max_exact + (np.log(np.maximum(n, 1).astype(np.float32) / np.float32(max_exact))
                         / np.float32(math.log(MAX_DISTANCE / max_exact))
                         * np.float32(nb - max_exact)).astype(np.int32)
    large = np.minimum(large, nb - 1)
    return (np.where(rel > 0, nb, 0) + np.where(n < max_exact, n, large)).astype(np.int32)


def _bucket_thresholds():
    dist = np.arange(0, 4 * MAX_DISTANCE, dtype=np.int32)
    bk = _t5_bucket_np(-dist)
    assert np.all(np.diff(bk) >= 0) and bk[-1] == N_BUCKETS // 2 - 1
    assert np.array_equal(_t5_bucket_np(dist[1:]), bk[1:] + N_BUCKETS // 2)
    return [int(np.argmax(bk >= k)) for k in range(N_BUCKETS // 2)]


def _attn_blocks(t, past):
    tq = min(ATTN_BLOCK, t)
    nq = t // tq
    assert t % tq == 0 and tq % CHUNK == 0
    if nq == 1:
        tk = past if past else tq
    else:
        assert past == 0
        tk = tq
    return tq, tk, nq


def _online_update(m, l, acc, s, shift, pv):
    m_new = jnp.maximum(m, jnp.max(s, axis=-1, keepdims=True) + shift)
    alpha = jnp.exp2(m - m_new)
    p = jnp.exp2(s - (m_new - shift))
    l = alpha * l + jnp.sum(p, axis=-1, keepdims=True)
    acc = alpha * acc + pv(p.astype(BF16))
    return m_new, l, acc


def _softmax_init(tq, width):
    return (jnp.full((tq, 1), NEG, F32), jnp.zeros((tq, 1), F32), jnp.zeros((tq, width), F32))


def _bias_kernel(rb_ref, near_ref, diag_ref, *, nh, thr):
    h = pl.program_id(0)
    half = N_BUCKETS // 2

    def tile(shape, col_off):
        r = lax.broadcasted_iota(jnp.int32, shape, 0)
        c = lax.broadcasted_iota(jnp.int32, shape, 1)
        rel = c + col_off - r
        dist = jnp.abs(rel)
        back = jnp.full(shape, rb_ref[h], F32)
        fwd = jnp.full(shape, rb_ref[half * nh + h], F32)
        for k in range(1, half):
            far = dist >= thr[k]
            back = jnp.where(far, rb_ref[k * nh + h], back)
            fwd = jnp.where(far, rb_ref[(half + k) * nh + h], fwd)
        return jnp.where(rel > 0, fwd, back) * LOG2E, r, c

    near, _, _ = tile(near_ref.shape[1:], -near_ref.shape[2])
    near_ref[0] = near
    diag, r, c = tile(diag_ref.shape[1:], 0)
    diag_ref[0] = jnp.where((c // CHUNK) <= (r // CHUNK), diag, NEG)


def _bias_tiles(rel_bias, tq, tk):
    nh = rel_bias.shape[1]
    return pl.pallas_call(
        functools.partial(_bias_kernel, nh=nh, thr=_bucket_thresholds()),
        grid=(nh,),
        in_specs=[pl.BlockSpec(memory_space=pltpu.SMEM)],
        out_specs=[pl.BlockSpec((1, tq, tk), lambda h: (h, 0, 0)),
                   pl.BlockSpec((1, tq, tq), lambda h: (h, 0, 0))],
        out_shape=[jax.ShapeDtypeStruct((nh, tq, tk), F32),
                   jax.ShapeDtypeStruct((nh, tq, tq), F32)],
        compiler_params=_cparams(1),
        name="rel_bias_tiles",
    )(rel_bias.reshape(-1))


def _diff_kernel(lam_ref, rb_ref, q_ref, kt_ref, v_ref, *rest, t, past, tq, tk, nh, out_scale):
    if past:
        kth_ref, vh_ref = rest[0], rest[1]
        rest = rest[2:]
    bn_ref, bd_ref, gm_ref, o_ref, kt_scr, v_scr = rest
    h = pl.program_id(1)
    qi = pl.program_id(2)
    nq = t // tq

    @pl.when(qi == 0)
    def _():
        if past:
            kt_scr[:, 0:past] = kth_ref[0, 0].astype(BF16)
            v_scr[0:past, :] = vh_ref[0, 0, pl.ds(h, past, stride=nh), :].astype(BF16)
        kt_scr[:, past:past + t] = kt_ref[0, 0].astype(BF16)
        v_scr[past:past + t, :] = v_ref[0, 0, pl.ds(h, t, stride=nh), :].astype(BF16)

    q = q_ref[0]
    lane = lax.broadcasted_iota(jnp.int32, q.shape, 1)
    q1 = jnp.where(lane < DQK_DIFF, q, jnp.zeros_like(q))
    q2 = jnp.where(lane >= DQK_DIFF, q, jnp.zeros_like(q))
    far_bias = rb_ref[(N_BUCKETS // 2 - 1) * nh + h] * LOG2E

    def block(carry, off, width, bias_ref):
        kt = kt_scr[:, pl.ds(off, width)]
        vb = v_scr[pl.ds(off, width), :]
        c1, c2 = carry
        pv = lambda p: _dot(p, vb)
        s1, s2 = _dot(q1, kt), _dot(q2, kt)
        if bias_ref is None:
            shift = far_bias
        else:
            shift = 0.0
            s1, s2 = s1 + bias_ref[0], s2 + bias_ref[0]
        return _online_update(*c1, s1, shift, pv), _online_update(*c2, s2, shift, pv)

    carry = (_softmax_init(tq, DV_DIFF), _softmax_init(tq, DV_DIFF))
    if nq == 1:
        if past:
            carry = block(carry, 0, past, bn_ref)
        d0 = past
    else:
        carry = lax.fori_loop(
            0, qi - 1, lambda j, cr: block(cr, pl.multiple_of(j * tk, tk), tk, None), carry)
        carry = lax.cond(qi >= 1,
                         lambda cr: block(cr, pl.multiple_of((qi - 1) * tk, tk), tk, bn_ref),
                         lambda cr: cr, carry)
        d0 = pl.multiple_of(qi * tq, tq)
    carry = block(carry, d0, tq, bd_ref)

    (_, l1, a1), (_, l2, a2) = carry
    o = a1 / l1 - lam_ref[0] * (a2 / l2)
    o_ref[0] = (_rms_rows(o) * out_scale * gm_ref[...]).astype(BF16)


def _diff_attn(q, kt, v, slot, kth, vh, li, lam, rel_bias, bias_tiles, gm, lam_init):
    bsz, t, hw = q.shape
    nh = hw // LANES
    past = 0 if kth is None else kth.shape[-1]
    tq, tk, nq = _attn_blocks(t, past)
    bias_near, bias_diag = bias_tiles
    smem = pl.BlockSpec(memory_space=pltpu.SMEM)
    args = [lam.reshape(1), rel_bias.reshape(-1), q, kt, v]
    in_specs = [smem, smem,
                pl.BlockSpec((1, tq, LANES), lambda b, h, i: (b, i, h)),
                pl.BlockSpec((1, 1, LANES, t), lambda b, h, i: (slot, b, h, 0)),
                pl.BlockSpec((1, 1, t * nh, LANES), lambda b, h, i: (slot, b, 0, 0))]
    if past:
        args += [kth, vh]
        in_specs += [pl.BlockSpec((1, 1, LANES, past), lambda b, h, i: (li, b, h, 0)),
                     pl.BlockSpec((1, 1, past * nh, LANES), lambda b, h, i: (li, b, 0, 0))]
    args += [bias_near, bias_diag, gm]
    in_specs += [pl.BlockSpec((1, tq, tk), lambda b, h, i: (h, 0, 0)),
                 pl.BlockSpec((1, tq, tq), lambda b, h, i: (h, 0, 0)),
                 pl.BlockSpec((1, LANES), lambda b, h, i: (0, h))]
    return pl.pallas_call(
        functools.partial(_diff_kernel, t=t, past=past, tq=tq, tk=tk, nh=nh, out_scale=1.0 - lam_init),
        grid=(bsz, nh, nq),
        in_specs=in_specs,
        out_specs=pl.BlockSpec((1, tq, LANES), lambda b, h, i: (b, i, h)),
        out_shape=jax.ShapeDtypeStruct((bsz, t, hw), BF16),
        scratch_shapes=[pltpu.VMEM((LANES, past + t), BF16),
                        pltpu.VMEM((past + t, LANES), BF16)],
        compiler_params=_cparams(3),
        name="diff_attn",
    )(*args)


def _logf_kernel(fg_ref, bf_ref, *rest, t, past, nh):
    if past:
        past_ref, logf_ref, cum_ref = rest
    else:
        logf_ref, cum_ref = rest
    x = fg_ref[0][:, :nh] + bf_ref[...]
    lf = -(jnp.maximum(-x, 0.0) + jnp.log1p(jnp.exp(-jnp.abs(x))))
    logf_ref[0] = lf

    def tri(n):
        r = lax.broadcasted_iota(jnp.int32, (n, n), 0)
        c = lax.broadcasted_iota(jnp.int32, (n, n), 1)
        return (r >= c).astype(F32)

    carry = jnp.zeros((1, nh), F32)
    pos = 0
    for src, length in ((None, past), (lf, t)):
        for r0 in range(0, length, LANES):
            n = min(LANES, length - r0)
            blk = past_ref[0, r0:r0 + n, :] if src is None else src[r0:r0 + n, :]
            cs = jnp.dot(tri(n), blk, preferred_element_type=F32,
                         precision=lax.Precision.HIGHEST) + carry
            cum_ref[0, pos:pos + n, :] = cs
            carry = cs[n - 1:n, :]
            pos += n


def _logf_cum(fg, b_f, past_logf):
    bsz, t, _ = fg.shape
    nh = b_f.shape[-1]
    past = 0 if past_logf is None else past_logf.shape[1]
    args = [fg, b_f.reshape(1, nh)]
    in_specs = [pl.BlockSpec((1, t, LANES), lambda b: (b, 0, 0)),
                pl.BlockSpec((1, nh), lambda b: (0, 0))]
    if past:
        args.append(past_logf)
        in_specs.append(pl.BlockSpec((1, past, nh), lambda b: (b, 0, 0)))
    return pl.pallas_call(
        functools.partial(_logf_kernel, t=t, past=past, nh=nh),
        grid=(bsz,),
        in_specs=in_specs,
        out_specs=[pl.BlockSpec((1, t, nh), lambda b: (b, 0, 0)),
                   pl.BlockSpec((1, past + t, nh), lambda b: (b, 0, 0))],
        out_shape=[jax.ShapeDtypeStruct((bsz, t, nh), F32),
                   jax.ShapeDtypeStruct((bsz, past + t, nh), F32)],
        compiler_params=_cparams(1),
        name="logf_cum",
    )(*args)


def _fox_kernel(q_ref, kt_ref, vt_ref, *rest, t, past, tq, tk):
    if past:
        kth_ref, vth_ref = rest[0], rest[1]
        rest = rest[2:]
    cq_ref, ck_ref, gm_ref, o_ref, kt_scr, vt_scr = rest
    hp = pl.program_id(1)
    qi = pl.program_id(2)
    nq = t // tq

    @pl.when(qi == 0)
    def _():
        if past:
            kt_scr[:, 0:past] = kth_ref[0, 0].astype(BF16)
            vt_scr[:, 0:past] = vth_ref[0, 0].astype(BF16)
        kt_scr[:, past:past + t] = kt_ref[0, 0].astype(BF16)
        vt_scr[:, past:past + t] = vt_ref[0, 0].astype(BF16)

    q = q_ref[0]
    lane = lax.broadcasted_iota(jnp.int32, q.shape, 1)
    cq_blk = cq_ref[0]
    head_lane = lax.broadcasted_iota(jnp.int32, cq_blk.shape, 1)
    r = lax.broadcasted_iota(jnp.int32, (tq, tq), 0)
    c = lax.broadcasted_iota(jnp.int32, (tq, tq), 1)
    causal = c <= r

    def head_row(blk, head):
        rid = lax.broadcasted_iota(jnp.int32, blk.shape, 0)
        return jnp.sum(jnp.where(rid == head, blk, 0.0), axis=0, keepdims=True)

    qh, cq = [], []
    for hh in range(2):
        in_half = (lane >= hh * HD_FOX) & (lane < (hh + 1) * HD_FOX)
        qh.append(jnp.where(in_half, q, jnp.zeros_like(q)))
        cq.append(jnp.sum(jnp.where(head_lane == 2 * hp + hh, cq_blk, 0.0), axis=-1, keepdims=True) * LOG2E)

    def block(carry, off, width, mask):
        kt = kt_scr[:, pl.ds(off, width)]
        vt = vt_scr[:, pl.ds(off, width)]
        ck_all = ck_ref[0, :, pl.ds(off, width)] * LOG2E
        pv = lambda p: _dot_nt(p, vt)
        out = []
        for hh in range(2):
            s = _dot(qh[hh], kt) - head_row(ck_all, 2 * hp + hh)
            if mask is not None:
                s = jnp.where(mask, s, NEG)
            out.append(_online_update(*carry[hh], s, cq[hh], pv))
        return tuple(out)

    carry = (_softmax_init(tq, LANES), _softmax_init(tq, LANES))
    if nq == 1:
        if past:
            carry = block(carry, 0, past, None)
        d0 = past
    else:
        carry = lax.fori_loop(
            0, qi, lambda j, cr: block(cr, pl.multiple_of(j * tk, tk), tk, None), carry)
        d0 = pl.multiple_of(qi * tq, tq)
    carry = block(carry, d0, tq, causal)

    (_, l0, a0), (_, l1, a1) = carry
    o = jnp.where(lane < HD_FOX, a0 / l0, a1 / l1)
    sq = o * o
    ms0 = jnp.sum(jnp.where(lane < HD_FOX, sq, 0.0), axis=-1, keepdims=True) / HD_FOX
    ms1 = jnp.sum(jnp.where(lane >= HD_FOX, sq, 0.0), axis=-1, keepdims=True) / HD_FOX
    inv = jnp.where(lane < HD_FOX, lax.rsqrt(ms0 + EPS), lax.rsqrt(ms1 + EPS))
    o_ref[0] = (o * inv * gm_ref[...]).astype(BF16)


def _fox_attn(q, kt, vt, slot, kth, vth, li, cum, gm):
    bsz, t, hw = q.shape
    npair = hw // LANES
    past = 0 if kth is None else kth.shape[-1]
    nh = cum.shape[-1]
    tq, tk, nq = _attn_blocks(t, past)
    cq = cum[:, past:, :]
    ck = jnp.swapaxes(cum, 1, 2)
    kv_spec = pl.BlockSpec((1, 1, LANES, t), lambda b, h, i: (slot, b, h, 0))
    args = [q, kt, vt]
    in_specs = [pl.BlockSpec((1, tq, LANES), lambda b, h, i: (b, i, h)), kv_spec, kv_spec]
    if past:
        h_spec = pl.BlockSpec((1, 1, LANES, past), lambda b, h, i: (li, b, h, 0))
        args += [kth, vth]
        in_specs += [h_spec, h_spec]
    args += [cq, ck, gm]
    in_specs += [pl.BlockSpec((1, tq, nh), lambda b, h, i: (b, i, 0)),
                 pl.BlockSpec((1, nh, past + t), lambda b, h, i: (b, 0, 0)),
                 pl.BlockSpec((1, LANES), lambda b, h, i: (0, h))]
    return pl.pallas_call(
        functools.partial(_fox_kernel, t=t, past=past, tq=tq, tk=tk),
        grid=(bsz, npair, nq),
        in_specs=in_specs,
        out_specs=pl.BlockSpec((1, tq, LANES), lambda b, h, i: (b, i, h)),
        out_shape=jax.ShapeDtypeStruct((bsz, t, hw), BF16),
        scratch_shapes=[pltpu.VMEM((LANES, past + t), BF16),
                        pltpu.VMEM((LANES, past + t), BF16)],
        compiler_params=_cparams(3),
        name="fox_attn",
    )(*args)


def _oproj_kernel(x_ref, a_ref, b_ref, c_ref, w_ref, o_ref, *, d1, d2):
    acc = _dot(a_ref[...], w_ref[0:d1, :])
    acc += _dot(b_ref[...], w_ref[d1:d2, :])
    acc += _dot(c_ref[...], w_ref[d2:, :])
    o_ref[...] = x_ref[...] + acc


def _oproj(x, a, b, c, w):
    n, d = x.shape
    tm = min(ROW_BLOCK, n)
    d1 = a.shape[1]
    d2 = d1 + b.shape[1]
    row = lambda width: pl.BlockSpec((tm, width), lambda i: (i, 0))
    return pl.pallas_call(
        functools.partial(_oproj_kernel, d1=d1, d2=d2),
        grid=(n // tm,),
        in_specs=[row(d), row(a.shape[1]), row(b.shape[1]), row(c.shape[1]), _resident(w.shape)],
        out_specs=row(d),
        out_shape=jax.ShapeDtypeStruct((n, d), F32),
        compiler_params=_cparams(1),
        name="oproj",
    )(x, a, b, c, w)


def _ffn_kernel(x_ref, g_ref, wa_ref, wb_ref, cw_ref, cb_ref, wo_ref, st_ref, o_ref, tail_ref,
                xn_scr, halo_scr, *, tm, t, tf):
    i = pl.program_id(0)
    c = pl.program_id(1)

    @pl.when(c == 0)
    def _():
        x = x_ref[...]
        xn_scr[...] = (_rms_rows(x) * g_ref[...]).astype(BF16)
        o_ref[...] = x

    tail_rows = 8
    if t >= tm:
        @pl.when((i % (t // tm)) == 0)
        def _():
            halo_scr[c] = st_ref[0]

        halo = halo_scr[c]
    else:
        st = st_ref[...]
        nseq = tm // t

    xn = xn_scr[...]
    rows8 = lax.broadcasted_iota(jnp.int32, (tail_rows, FF_SUB), 0)
    rows = lax.broadcasted_iota(jnp.int32, (tm, FF_SUB), 0)
    gated = []
    for k in range(tf // FF_SUB):
        cols = slice(k * FF_SUB, (k + 1) * FF_SUB)
        a = _dot(xn, wa_ref[:, cols])
        b = _dot(xn, wb_ref[:, cols])
        prev1 = pltpu.roll(a, 1, 0)
        prev2 = pltpu.roll(a, 2, 0)
        if t >= tm:
            h0 = halo[tail_rows - 2:tail_rows - 1, cols]
            h1 = halo[tail_rows - 1:tail_rows, cols]
            head1 = jnp.where(rows8 == 0, h1, prev1[:tail_rows])
            head2 = jnp.where(rows8 == 0, h0, jnp.where(rows8 == 1, h1, prev2[:tail_rows]))
            a1 = jnp.concatenate([head1, prev1[tail_rows:]], axis=0)
            a2 = jnp.concatenate([head2, prev2[tail_rows:]], axis=0)
            last = a[tm - tail_rows:tm, :]
            halo_scr[c, :, cols] = last
            tail_ref[0, :, cols] = last
        else:
            h0 = jnp.broadcast_to(st[:, tail_rows - 2:tail_rows - 1, cols], (nseq, t, FF_SUB)).reshape(tm, FF_SUB)
            h1 = jnp.broadcast_to(st[:, tail_rows - 1:tail_rows, cols], (nseq, t, FF_SUB)).reshape(tm, FF_SUB)
            rmod = rows & (t - 1)
            a1 = jnp.where(rmod == 0, h1, prev1)
            a2 = jnp.where(rmod == 0, h0, jnp.where(rmod == 1, h1, prev2))
            tail_ref[:, :, cols] = a.reshape(nseq, t, FF_SUB)[:, t - tail_rows:, :]
        ac = cw_ref[2:3, cols] * a + cw_ref[1:2, cols] * a1 + cw_ref[0:1, cols] * a2 + cb_ref[:, cols]
        gated.append((ac * (1.0 / (1.0 + jnp.exp(-ac))) * b).astype(BF16))
    o_ref[...] += _dot(jnp.concatenate(gated, axis=1), wo_ref[...])


def _ffn(x, t, g, wa, wb, cw, cb, wo, st):
    n, d = x.shape
    ffp = wo.shape[0]
    tf = min(FF_BLOCK, ffp)
    tm = min(FF_ROWS, n)
    assert ffp % tf == 0 and tf % FF_SUB == 0 and n % tm == 0 and (t % tm == 0 or tm % t == 0)
    assert t & (t - 1) == 0
    nseq = n // t
    if t >= tm:
        per = t // tm
        st_spec = pl.BlockSpec((1, 8, tf), lambda i, c: (i // per, 0, c))
        tail_spec = pl.BlockSpec((1, 8, tf), lambda i, c: (i, 0, c))
    else:
        per = 1
        st_spec = tail_spec = pl.BlockSpec((tm // t, 8, tf), lambda i, c: (i, 0, c))
    x_new, tail = pl.pallas_call(
        functools.partial(_ffn_kernel, tm=tm, t=t, tf=tf),
        grid=(n // tm, ffp // tf),
        in_specs=[pl.BlockSpec((tm, d), lambda i, c: (i, 0)),
                  pl.BlockSpec((1, d), lambda i, c: (0, 0)),
                  pl.BlockSpec((d, tf), lambda i, c: (0, c)),
                  pl.BlockSpec((d, tf), lambda i, c: (0, c)),
                  pl.BlockSpec((8, tf), lambda i, c: (0, c)),
                  pl.BlockSpec((1, tf), lambda i, c: (0, c)),
                  pl.BlockSpec((tf, d), lambda i, c: (c, 0)),
                  st_spec],
        out_specs=[pl.BlockSpec((tm, d), lambda i, c: (i, 0)), tail_spec],
        out_shape=[jax.ShapeDtypeStruct((n, d), F32),
                   jax.ShapeDtypeStruct((nseq * per, 8, ffp), F32)],
        scratch_shapes=[pltpu.VMEM((tm, d), BF16),
                        pltpu.VMEM((ffp // tf, 8, tf), F32)],
        compiler_params=_cparams(2),
        name="ffn",
    )(x, g, wa, wb, cw, cb, wo, st)
    return x_new, tail.reshape(nseq, per, 8, ffp)[:, per - 1]


def _norm_kernel(x_ref, g_ref, o_ref):
    o_ref[...] = _rms_rows(x_ref[...]) * g_ref[...]


def _final_norm(x, g):
    n, d = x.shape
    tm = min(ROW_BLOCK, n)
    return pl.pallas_call(
        _norm_kernel,
        grid=(n // tm,),
        in_specs=[pl.BlockSpec((tm, d), lambda i: (i, 0)), pl.BlockSpec((1, d), lambda i: (0, 0))],
        out_specs=pl.BlockSpec((tm, d), lambda i: (i, 0)),
        out_shape=jax.ShapeDtypeStruct((n, d), F32),
        compiler_params=_cparams(1),
        name="final_norm",
    )(x, g)


def _prep_layer(p, dims):
    d_ssm, hd, hf, d_ff, ffp = dims
    wd, wf = hd * LANES, hf * HD_FOX
    o_qd, o_kd, o_vd = d_ssm, d_ssm + wd, d_ssm + 2 * wd
    o_qf, o_kf, o_vf, o_fg = d_ssm + 3 * wd, d_ssm + 3 * wd + wf, d_ssm + 3 * wd + 2 * wf, d_ssm + 3 * wd + 3 * wf
    w_in = p['w_in']
    w_fg = jnp.pad(w_in[:, o_fg:], ((0, 0), (0, LANES - hf)))
    q = dict(p)
    q['w_in_n'] = jnp.concatenate([w_in[:, o_qd:o_kd], w_in[:, o_qf:o_kf], w_fg, w_in[:, o_vd:o_qf]],
                                  axis=1).astype(BF16)
    q['w_in_t'] = jnp.concatenate([w_in[:, :o_qd], w_in[:, o_kd:o_vd], w_in[:, o_kf:o_fg]],
                                  axis=1).T.astype(BF16)
    q['w_glu'] = p['w_glu'].astype(BF16)
    q['w_o'] = p['w_o'].astype(BF16)
    padc = ((0, 0), (0, ffp - d_ff))
    q['w_a'] = jnp.pad(p['w_ffn_in'][:, :d_ff], padc).astype(BF16)
    q['w_b'] = jnp.pad(p['w_ffn_in'][:, d_ff:], padc).astype(BF16)
    q['w_ffn_out'] = jnp.pad(p['w_ffn_out'], ((0, ffp - d_ff), (0, 0))).astype(BF16)
    q['conv_w'] = jnp.pad(p['ffn_conv_w'], ((0, 8 - CONV_W), (0, ffp - d_ff)))
    q['conv_b'] = jnp.pad(p['ffn_conv_b'], (0, ffp - d_ff))[None, :]
    s5 = _s5_params(p['ssm_lam_re'], p['ssm_lam_im'], p['ssm_log_dt'], p['ssm_b_re'], p['ssm_b_im'],
                    p['ssm_c_re'], p['ssm_c_im'], p['ssm_d'], (S5_BLOCK, CHUNK))
    q['s5_seq'], q['s5_short'] = s5[S5_BLOCK], s5[CHUNK]
    q['lam'] = (jnp.exp(jnp.sum(p['diff_lam_q1'] * p['diff_lam_k1']))
                - jnp.exp(jnp.sum(p['diff_lam_q2'] * p['diff_lam_k2'])) + p['lam_init'])
    return q


def _layer(x, bsz, t, li, hist, st, p, dims, bias_tiles, lam_init, slot, nslots, stacks):
    d_ssm, hd, hf, d_ff, ffp = dims
    kdt_h, vd_h, kft_h, vft_h = hist
    lf_c, sr_c, si_c, cv_c = st
    n, d = x.shape
    wd, wf = hd * LANES, hf * HD_FOX
    u, qd, qf, fg, kv = _inproj(x, t, p['g_norm_mix'][None, :], p['w_in_n'], p['w_in_t'],
                                d_ssm, wd, wf, slot, nslots, stacks)
    vd, kdt, kft, vft = kv
    gm = p['g_mix_out'][None, :]
    if t >= min(ROW_BLOCK, n):
        y_ssm, sr, si = _s5_seq(u, sr_c, si_c, p['s5_seq'])
    else:
        y_ssm, sr, si = _s5_short(u[0], sr_c, si_c, p['s5_short'])
    o_ssm = _glu(y_ssm, p['w_glu'], p['b_glu'][None, :], gm[:, :d_ssm])

    o_diff = _diff_attn(qd.reshape(bsz, t, wd), kdt, vd.reshape(nslots, bsz, t * hd, DV_DIFF), slot,
                        kdt_h, vd_h, li, p['lam'], p['rel_bias'], bias_tiles,
                        gm[:, d_ssm:d_ssm + wd], lam_init)

    logf, cum = _logf_cum(fg.reshape(bsz, t, LANES), p['fox_b_f'], lf_c)
    o_fox = _fox_attn(qf.reshape(bsz, t, wf), kft, vft, slot, kft_h, vft_h, li, cum, gm[:, d_ssm + wd:])

    x = _oproj(x, o_ssm, o_diff.reshape(n, wd), o_fox.reshape(n, wf), p['w_o'])

    x, tail = _ffn(x, t, p['g_norm_ffn'][None, :], p['w_a'], p['w_b'], p['conv_w'], p['conv_b'],
                   p['w_ffn_out'], cv_c)
    return x, kv, (logf, sr, si, tail[:, 8 - (CONV_W - 1):, :d_ff])


def kernel(x_prompt, x_sample, cache_diff_k, cache_diff_v, cache_fox_k, cache_fox_v, cache_fox_logf, state_ssm_re, state_ssm_im, state_ffn_conv, g_norm_mix, w_in, ssm_lam_re, ssm_lam_im, ssm_log_dt, ssm_b_re, ssm_b_im, ssm_c_re, ssm_c_im, ssm_d, w_glu, b_glu, diff_lam_q1, diff_lam_k1, diff_lam_q2, diff_lam_k2, rel_bias, fox_b_f, g_mix_out, w_o, g_norm_ffn, w_ffn_in, ffn_conv_w, ffn_conv_b, w_ffn_out, g_final):
    per_layer = {'g_norm_mix': g_norm_mix, 'w_in': w_in, 'ssm_lam_re': ssm_lam_re,
                 'ssm_lam_im': ssm_lam_im, 'ssm_log_dt': ssm_log_dt, 'ssm_b_re': ssm_b_re,
                 'ssm_b_im': ssm_b_im, 'ssm_c_re': ssm_c_re, 'ssm_c_im': ssm_c_im, 'ssm_d': ssm_d,
                 'w_glu': w_glu, 'b_glu': b_glu, 'diff_lam_q1': diff_lam_q1, 'diff_lam_k1': diff_lam_k1,
                 'diff_lam_q2': diff_lam_q2, 'diff_lam_k2': diff_lam_k2, 'fox_b_f': fox_b_f,
                 'g_mix_out': g_mix_out, 'w_o': w_o, 'g_norm_ffn': g_norm_ffn, 'w_ffn_in': w_ffn_in,
                 'ffn_conv_w': ffn_conv_w, 'ffn_conv_b': ffn_conv_b, 'w_ffn_out': w_ffn_out}
    depth = w_in.shape[0]
    bp, tp, d_model = x_prompt.shape
    bs, ts, _ = x_sample.shape
    past = cache_diff_k.shape[2]
    g_ssm = ssm_lam_re.shape[1]
    d_ssm = g_ssm * SSM_GROUP
    hd = cache_diff_k.shape[3]
    hf = cache_fox_k.shape[3]
    wd, wf = hd * LANES, hf * HD_FOX
    d_ff = w_ffn_out.shape[1]
    ffp = -(-d_ff // FF_BLOCK) * FF_BLOCK if d_ff > FF_BLOCK else d_ff
    dims = (d_ssm, hd, hf, d_ff, ffp)

    def conv_state_block(s):
        return jnp.pad(s, ((0, 0), (8 - (CONV_W - 1), 0), (0, ffp - d_ff)))

    hist_sample = (jnp.transpose(cache_diff_k, (0, 1, 3, 4, 5, 2)).reshape(depth, bs, wd, past),
                   cache_diff_v.reshape(depth, bs, past * hd, DV_DIFF),
                   jnp.transpose(cache_fox_k, (0, 1, 3, 4, 2)).reshape(depth, bs, wf, past),
                   jnp.transpose(cache_fox_v, (0, 1, 3, 4, 2)).reshape(depth, bs, wf, past))
    hist_prompt = (None, None, None, None)
    st_prompt = (None, jnp.zeros((bp, g_ssm, P_SSM), F32), jnp.zeros((bp, g_ssm, P_SSM), F32),
                 conv_state_block(jnp.zeros((bp, CONV_W - 1, d_ff), F32)))
    bias_p = _bias_tiles(rel_bias, *_attn_blocks(tp, 0)[:2])
    bias_s = _bias_tiles(rel_bias, *_attn_blocks(ts, past)[:2])

    xp = x_prompt.reshape(bp * tp, d_model)
    xs = x_sample.reshape(bs * ts, d_model)
    kv_p, kv_s, outs_p, outs_s = None, [], [], []
    for li in range(depth):
        lam_init = 0.8 - 0.6 * math.exp(-0.3 * li)
        p = {name: arr[li] for name, arr in per_layer.items()}
        p['rel_bias'] = rel_bias
        p['lam_init'] = lam_init
        p = _prep_layer(p, dims)
        xp, kv_p, sp = _layer(xp, bp, tp, li, hist_prompt, st_prompt, p, dims, bias_p, lam_init,
                              li, depth, kv_p)
        st_sample = (cache_fox_logf[li], state_ssm_re[li], state_ssm_im[li],
                     conv_state_block(state_ffn_conv[li]))
        xs, kv, ss = _layer(xs, bs, ts, li, hist_sample, st_sample, p, dims, bias_s, lam_init,
                            0, 1, None)
        kv_s.append(kv)
        outs_p.append(sp)
        outs_s.append(ss)
    kv_s = [jnp.concatenate([kv[i] for kv in kv_s]) for i in range(4)]

    def assemble(kv, outs, bsz, t):
        vd, kdt, kft, vft = kv
        logf, sr, si, cv = [jnp.stack([s[i] for s in outs]) for i in range(4)]
        return (jnp.transpose(kdt.reshape(depth, bsz, hd, 2, DQK_DIFF, t), (0, 1, 5, 2, 3, 4)),
                vd.reshape(depth, bsz, t, hd, DV_DIFF),
                jnp.transpose(kft.reshape(depth, bsz, hf, HD_FOX, t), (0, 1, 4, 2, 3)),
                jnp.transpose(vft.reshape(depth, bsz, hf, HD_FOX, t), (0, 1, 4, 2, 3)),
                logf, sr, si, cv)

    y_prompt = _final_norm(xp, g_final[None, :]).reshape(bp, tp, d_model)
    y_sample = _final_norm(xs, g_final[None, :]).reshape(bs, ts, d_model)
    return (y_prompt, y_sample, *assemble(kv_p, outs_p, bp, tp), *assemble(kv_s, outs_s, bs, ts))
```

```python
import functools
import math

import numpy as np
import jax
import jax.numpy as jnp
from jax import lax
from jax.experimental import pallas as pl
from jax.experimental.pallas import tpu as pltpu

F32 = jnp.float32
BF16 = jnp.bfloat16

EPS = 1e-6
CHUNK = 64
S5_BLOCK = 128
SSM_GROUP = 16
P_SSM = 64
DQK_DIFF = 64
DV_DIFF = 128
HD_FOX = 64
N_BUCKETS = 32
MAX_DISTANCE = 128
CONV_W = 3
LANES = 128
ATTN_BLOCK = 1024
FF_BLOCK = 512
FF_SUB = 256
ROW_BLOCK = 512
FF_ROWS = 1024
V7X_VMEM_LIMIT = 56 * 1024 * 1024
NEG = -1e30
LOG2E = math.log2(math.e)


def _cparams(n_axes):
    return pltpu.CompilerParams(dimension_semantics=("arbitrary",) * n_axes,
                                vmem_limit_bytes=V7X_VMEM_LIMIT)


def _resident(shape):
    nd = len(shape)
    return pl.BlockSpec(shape, lambda *_: (0,) * nd, pipeline_mode=pl.Buffered(1))


def _rms_rows(x):
    return x * lax.rsqrt(jnp.mean(x * x, axis=-1, keepdims=True) + EPS)


def _dot(a, b):
    return jnp.dot(a, b, preferred_element_type=F32)


def _dot_nt(a, b):
    return lax.dot_general(a, b, (((1,), (1,)), ((), ())), preferred_element_type=F32)


def _inproj_kernel(x_ref, g_ref, wn_ref, wt_ref, *refs, d_ssm, wd, wf, ns, u_slabs):
    u_ref, qd_ref, qf_ref, fg_ref, vd_ref, kdt_ref, kft_ref, vft_ref = refs[-8:]
    tm = x_ref.shape[0]
    xn = (_rms_rows(x_ref[...]) * g_ref[...]).astype(BF16)
    c = 0
    qd_ref[...] = (_dot(xn, wn_ref[:, c:c + wd]) * (DQK_DIFF ** -0.5 * LOG2E)).astype(BF16)
    c += wd
    qf_ref[...] = (_dot(xn, wn_ref[:, c:c + wf]) * (HD_FOX ** -0.5 * LOG2E)).astype(BF16)
    c += wf
    fg_ref[...] = _dot(xn, wn_ref[:, c:c + LANES])
    c += LANES
    vd = _dot(xn, wn_ref[:, c:c + wd])
    nh = wd // DV_DIFF
    for h in range(nh):
        vd_ref[0, pl.ds(h, tm, stride=nh), :] = vd[:, h * DV_DIFF:(h + 1) * DV_DIFF]
    ts = tm // ns
    zt = _dot_nt(wt_ref[0:d_ssm, :], xn)
    if u_slabs:
        zg = zt.reshape(d_ssm // SSM_GROUP, SSM_GROUP, tm)
        for k in range(u_slabs):
            u_ref[:, k, 0] = zg[:, :, k * S5_BLOCK:(k + 1) * S5_BLOCK]
    else:
        for s in range(ns):
            u_ref[0, s] = zt[:, s * ts:(s + 1) * ts]
    r = d_ssm
    for o_ref, width in ((kdt_ref, wd), (kft_ref, wf), (vft_ref, wf)):
        zt = _dot_nt(wt_ref[r:r + width, :], xn)
        for s in range(ns):
            o_ref[0, s] = zt[:, s * ts:(s + 1) * ts]
        r += width


def _inproj(x, t, g, wn, wt, d_ssm, wd, wf, slot, nslots, stacks):
    n, d = x.shape
    tm = min(ROW_BLOCK, n)
    nseq = n // t
    nh = wd // DV_DIFF
    assert n % tm == 0 and (t % tm == 0 or tm % t == 0)
    if t >= tm:
        per, ns = t // tm, 1
        tspec = lambda width: pl.BlockSpec((1, 1, width, tm), lambda i: (slot, i // per, 0, i % per))
        u_slabs = tm // S5_BLOCK
        g_ssm = d_ssm // SSM_GROUP
        u_spec = pl.BlockSpec((g_ssm, u_slabs, 1, SSM_GROUP, S5_BLOCK),
                              lambda i: (0, i % per, i // per, 0, 0))
        u_shape = (g_ssm, t // S5_BLOCK, nseq, SSM_GROUP, S5_BLOCK)
    else:
        ns = tm // t
        tspec = lambda width: pl.BlockSpec((1, ns, width, t), lambda i: (slot, i, 0, 0))
        u_slabs = 0
        u_spec = pl.BlockSpec((1, ns, d_ssm, t), lambda i: (0, i, 0, 0))
        u_shape = (1, nseq, d_ssm, t)
    row = lambda width: pl.BlockSpec((tm, width), lambda i: (i, 0))
    in_specs = [row(d), _resident((1, d)), _resident(wn.shape), _resident(wt.shape)]
    args = [x, g, wn, wt]
    aliases = {}
    if stacks is not None:
        in_specs += [pl.BlockSpec(memory_space=pl.ANY)] * len(stacks)
        aliases = {len(args) + k: 4 + k for k in range(len(stacks))}
        args += list(stacks)
    outs = pl.pallas_call(
        functools.partial(_inproj_kernel, d_ssm=d_ssm, wd=wd, wf=wf, ns=ns, u_slabs=u_slabs),
        grid=(n // tm,),
        in_specs=in_specs,
        out_specs=[u_spec, row(wd), row(wf), row(LANES),
                   pl.BlockSpec((1, tm * nh, DV_DIFF), lambda i: (slot, i, 0)),
                   tspec(wd), tspec(wf), tspec(wf)],
        out_shape=[jax.ShapeDtypeStruct(u_shape, F32),
                   jax.ShapeDtypeStruct((n, wd), BF16),
                   jax.ShapeDtypeStruct((n, wf), BF16),
                   jax.ShapeDtypeStruct((n, LANES), F32),
                   jax.ShapeDtypeStruct((nslots, n * nh, DV_DIFF), F32),
                   jax.ShapeDtypeStruct((nslots, nseq, wd, t), F32),
                   jax.ShapeDtypeStruct((nslots, nseq, wf, t), F32),
                   jax.ShapeDtypeStruct((nslots, nseq, wf, t), F32)],
        input_output_aliases=aliases,
        compiler_params=_cparams(1),
        name="inproj",
    )(*args)
    return outs[0], outs[1], outs[2], outs[3], tuple(outs[4:])


def _s5_block_ops(u, kt_ref, p_ref, q_ref, a_ref, d_ref, h0_ref, ht_ref, m_scr, v_scr, hin_scr,
                  nb, bsz, blk, emit):
    width = SSM_GROUP * blk
    lane = lax.broadcasted_iota(jnp.int32, (blk, width), 1)
    row = lax.broadcasted_iota(jnp.int32, (blk, width), 0)
    causal = (lane & (blk - 1)) >= row
    for h in range(SSM_GROUP):
        base = jnp.broadcast_to(kt_ref[0, h:h + 1, :], (blk, width))
        shifted = pltpu.roll(base, 0, 1, stride=1, stride_axis=0)
        m_scr[h * blk:(h + 1) * blk, :] = jnp.where(causal, shifted, 0.0).astype(BF16)

    ub = u.astype(BF16)
    emit(_dot(ub, m_scr[...]) + d_ref[0] * u, None)
    v_scr[...] = _dot(ub, p_ref[0])

    a_same = a_ref[0, 0:1, :]
    a_cross = a_ref[0, 1:2, :]
    h = h0_ref[0]
    for c in range(nb):
        hin_scr[c * bsz:(c + 1) * bsz, :] = h
        h = a_same * h + a_cross * pltpu.roll(h, P_SSM, 1) + v_scr[c * bsz:(c + 1) * bsz, :]
    ht_ref[0] = h
    emit(None, _dot(hin_scr[...].astype(BF16), q_ref[0]))


def _s5_kernel(u_ref, kt_ref, p_ref, q_ref, a_ref, d_ref, h0_ref, y_ref, ht_ref,
               m_scr, v_scr, hin_scr, *, nb, bsz, blk):
    def emit(y_local, y_state):
        if y_local is not None:
            y_ref[0] = y_local
        else:
            y_ref[0] += y_state

    _s5_block_ops(u_ref[0], kt_ref, p_ref, q_ref, a_ref, d_ref, h0_ref, ht_ref, m_scr, v_scr, hin_scr,
                  nb, bsz, blk, emit)


def _s5_seq_kernel(u_ref, kt_ref, p_ref, q_ref, a_ref, d_ref, h0_ref, y_ref, ht_ref,
                   m_scr, v_scr, hin_scr, u_scr, y_scr, *, nb, bsz, blk):
    rows = nb * bsz
    for h in range(SSM_GROUP):
        u_scr[:, h * blk:(h + 1) * blk] = u_ref[0, pl.ds(h, rows, stride=SSM_GROUP), :]

    def emit(y_local, y_state):
        if y_local is not None:
            y_scr[...] = y_local
        else:
            y_scr[...] += y_state

    _s5_block_ops(u_scr[...], kt_ref, p_ref, q_ref, a_ref, d_ref, h0_ref, ht_ref, m_scr, v_scr, hin_scr,
                  nb, bsz, blk, emit)
    for h in range(SSM_GROUP):
        y_ref[0, pl.ds(h, rows, stride=SSM_GROUP), :] = y_scr[:, h * blk:(h + 1) * blk]


def _s5_params(lam_re, lam_im, log_dt, b_re, b_im, c_re, c_im, d_skip, blks):
    hp = lax.Precision.HIGHEST
    g = lam_re.shape[0]
    top = max(blks)
    dt = jnp.exp(log_dt)[:, None]
    k = jnp.arange(0, top + 1, dtype=F32)[:, None, None]
    mag = jnp.exp(lam_re * dt * k)
    pr = mag * jnp.cos(lam_im * dt * k)
    pi = mag * jnp.sin(lam_im * dt * k)
    ar, ai = pr[1], pi[1]
    den = lam_re * lam_re + lam_im * lam_im
    fr = ((ar - 1.0) * lam_re + ai * lam_im) / den
    fi = (ai * lam_re - (ar - 1.0) * lam_im) / den
    bbr = fr[..., None] * b_re - fi[..., None] * b_im
    bbi = fr[..., None] * b_im + fi[..., None] * b_re
    car = c_re[None] * pr[:, :, None, :] - c_im[None] * pi[:, :, None, :]
    cai = c_re[None] * pi[:, :, None, :] + c_im[None] * pr[:, :, None, :]
    kt_top = (jnp.einsum('kgap,gph->ghak', car[:top], bbr, precision=hp)
              - jnp.einsum('kgap,gph->ghak', cai[:top], bbi, precision=hp))
    out = {}
    for blk in blks:
        kt = kt_top[..., :blk].reshape(g, SSM_GROUP, SSM_GROUP * blk)
        prr = pr[blk - 1::-1][:blk]
        pir = pi[blk - 1::-1][:blk]
        p_re = prr[:, :, :, None] * bbr[None] - pir[:, :, :, None] * bbi[None]
        p_im = prr[:, :, :, None] * bbi[None] + pir[:, :, :, None] * bbr[None]
        pmat = jnp.concatenate([jnp.transpose(p_re, (1, 3, 0, 2)), jnp.transpose(p_im, (1, 3, 0, 2))],
                               axis=-1).reshape(g, SSM_GROUP * blk, 2 * P_SSM)
        q_re = jnp.transpose(car[1:blk + 1], (1, 3, 2, 0))
        q_im = -jnp.transpose(cai[1:blk + 1], (1, 3, 2, 0))
        qmat = jnp.concatenate([q_re, q_im], axis=1).reshape(g, 2 * P_SSM, SSM_GROUP * blk)
        a_blk = jnp.stack([jnp.concatenate([pr[blk], pr[blk]], -1),
                           jnp.concatenate([-pi[blk], pi[blk]], -1)], axis=1)
        dvec = jnp.repeat(d_skip, blk, axis=-1)[:, None, :]
        out[blk] = (kt, pmat.astype(BF16), qmat.astype(BF16), a_blk, dvec)
    return out


def _s5_call(kern, u_arr, u_spec, sp, h_re, h_im, nb, bsz, blk, extra_scratch):
    kt, pmat, qmat, a_blk, dvec = sp
    g = kt.shape[0]
    width = SSM_GROUP * blk
    rows = nb * bsz
    h0 = jnp.transpose(jnp.concatenate([h_re, h_im], axis=-1), (1, 0, 2))
    y, ht = pl.pallas_call(
        functools.partial(kern, nb=nb, bsz=bsz, blk=blk),
        grid=(g,),
        in_specs=[u_spec,
                  pl.BlockSpec((1, SSM_GROUP, width), lambda i: (i, 0, 0)),
                  pl.BlockSpec((1, width, 2 * P_SSM), lambda i: (i, 0, 0)),
                  pl.BlockSpec((1, 2 * P_SSM, width), lambda i: (i, 0, 0)),
                  pl.BlockSpec((1, 2, 2 * P_SSM), lambda i: (i, 0, 0)),
                  pl.BlockSpec((1, 1, width), lambda i: (i, 0, 0)),
                  pl.BlockSpec((1, bsz, 2 * P_SSM), lambda i: (i, 0, 0))],
        out_specs=[u_spec, pl.BlockSpec((1, bsz, 2 * P_SSM), lambda i: (i, 0, 0))],
        out_shape=[jax.ShapeDtypeStruct(u_arr.shape, F32),
                   jax.ShapeDtypeStruct((g, bsz, 2 * P_SSM), F32)],
        scratch_shapes=[pltpu.VMEM((width, width), BF16),
                        pltpu.VMEM((rows, 2 * P_SSM), F32),
                        pltpu.VMEM((rows, 2 * P_SSM), F32)] + extra_scratch,
        compiler_params=_cparams(1),
        name="s5",
    )(u_arr, kt, pmat, qmat, a_blk, dvec, h0)
    ht = jnp.transpose(ht, (1, 0, 2))
    return y, ht[..., :P_SSM], ht[..., P_SSM:]


def _s5_short(ut, h_re, h_im, sp):
    bsz, dssm, t = ut.shape
    g = dssm // SSM_GROUP
    nb = t // CHUNK
    width = SSM_GROUP * CHUNK
    rows = nb * bsz
    u = jnp.transpose(ut.reshape(bsz, g, SSM_GROUP, nb, CHUNK), (1, 3, 0, 2, 4)).reshape(g, rows, width)
    y, sr, si = _s5_call(_s5_kernel, u, pl.BlockSpec((1, rows, width), lambda i: (i, 0, 0)),
                         sp, h_re, h_im, nb, bsz, CHUNK, [])
    y = jnp.transpose(y.reshape(g, nb, bsz, SSM_GROUP, CHUNK), (2, 1, 4, 0, 3)).reshape(bsz * t, dssm)
    return y, sr, si


def _s5_seq(u5, h_re, h_im, sp):
    g, nb, bsz, _, blk = u5.shape
    rows, width = nb * bsz, SSM_GROUP * blk
    spec = pl.BlockSpec((1, rows * SSM_GROUP, blk), lambda i: (i, 0, 0))
    y, sr, si = _s5_call(_s5_seq_kernel, u5.reshape(g, rows * SSM_GROUP, blk), spec, sp, h_re, h_im,
                         nb, bsz, blk, [pltpu.VMEM((rows, width), F32), pltpu.VMEM((rows, width), F32)])
    return y.reshape(u5.shape), sr, si


def _glu_math(y, w_ref, b_ref, gm_ref):
    g = 0.5 * y * (1.0 + jnp.tanh(math.sqrt(2.0 / math.pi) * (y + 0.044715 * (y * y * y))))
    z = _dot(g.astype(BF16), w_ref[...]) + b_ref[...]
    yy = g * (1.0 / (1.0 + jnp.exp(-z)))
    return (_rms_rows(yy) * gm_ref[...]).astype(BF16)


def _glu_kernel(y_ref, w_ref, b_ref, gm_ref, o_ref):
    o_ref[...] = _glu_math(y_ref[...], w_ref, b_ref, gm_ref)


def _glu_seq_kernel(y_ref, w_ref, b_ref, gm_ref, o_ref, y_scr):
    g, nk, _, hch, blk = y_ref.shape
    for k in range(nk):
        y_scr[k * blk:(k + 1) * blk, :] = jnp.transpose(y_ref[:, k, 0].reshape(g * hch, blk))
    o_ref[...] = _glu_math(y_scr[...], w_ref, b_ref, gm_ref)


def _glu(y, w, b, gm):
    if y.ndim == 2:
        n, d = y.shape
        tm = min(ROW_BLOCK, n)
        kern, y_spec, scratch = _glu_kernel, pl.BlockSpec((tm, d), lambda i: (i, 0)), []
    else:
        g, nb, bsz, hch, blk = y.shape
        d = g * hch
        n, tm = nb * bsz * blk, ROW_BLOCK
        per = nb * blk // tm
        kern = _glu_seq_kernel
        y_spec = pl.BlockSpec((g, tm // blk, 1, hch, blk), lambda i: (0, i % per, i // per, 0, 0))
        scratch = [pltpu.VMEM((tm, d), F32)]
    return pl.pallas_call(
        kern,
        grid=(n // tm,),
        in_specs=[y_spec, _resident(w.shape), _resident((1, d)), _resident((1, d))],
        out_specs=pl.BlockSpec((tm, d), lambda i: (i, 0)),
        out_shape=jax.ShapeDtypeStruct((n, d), BF16),
        scratch_shapes=scratch,
        compiler_params=_cparams(1),
        name="glu",
    )(y, w, b, gm)


def _t5_bucket_np(rel):
    nb = N_BUCKETS // 2
    max_exact = nb // 2
    n = np.abs(rel)
    large = max_exact + (np.log(np.maximum(n, 1).astype(np.float32) / np.float32(max_exact))
                         / np.float32(math.log(MAX_DISTANCE / max_exact))
                         * np.float32(nb - max_exact)).astype(np.int32)
    large = np.minimum(large, nb - 1)
    return (np.where(rel > 0, nb, 0) + np.where(n < max_exact, n, large)).astype(np.int32)


def _bucket_thresholds():
    dist = np.arange(0, 4 * MAX_DISTANCE, dtype=np.int32)
    bk = _t5_bucket_np(-dist)
    assert np.all(np.diff(bk) >= 0) and bk[-1] == N_BUCKETS // 2 - 1
    assert np.array_equal(_t5_bucket_np(dist[1:]), bk[1:] + N_BUCKETS // 2)
    return [int(np.argmax(bk >= k)) for k in range(N_BUCKETS // 2)]


def _attn_blocks(t, past):
    tq = min(ATTN_BLOCK, t)
    nq = t // tq
    assert t % tq == 0 and tq % CHUNK == 0
    if nq == 1:
        tk = past if past else tq
    else:
        assert past == 0
        tk = tq
    return tq, tk, nq


def _online_update(m, l, acc, s, shift, pv):
    m_new = jnp.maximum(m, jnp.max(s, axis=-1, keepdims=True) + shift)
    alpha = jnp.exp2(m - m_new)
    p = jnp.exp2(s - (m_new - shift))
    l = alpha * l + jnp.sum(p, axis=-1, keepdims=True)
    acc = alpha * acc + pv(p.astype(BF16))
    return m_new, l, acc


def _softmax_init(tq, width):
    return (jnp.full((tq, 1), NEG, F32), jnp.zeros((tq, 1), F32), jnp.zeros((tq, width), F32))


def _bias_kernel(rb_ref, near_ref, diag_ref, *, nh, thr):
    h = pl.program_id(0)
    half = N_BUCKETS // 2

    def tile(shape, col_off):
        r = lax.broadcasted_iota(jnp.int32, shape, 0)
        c = lax.broadcasted_iota(jnp.int32, shape, 1)
        rel = c + col_off - r
        dist = jnp.abs(rel)
        back = jnp.full(shape, rb_ref[h], F32)
        fwd = jnp.full(shape, rb_ref[half * nh + h], F32)
        for k in range(1, half):
            far = dist >= thr[k]
            back = jnp.where(far, rb_ref[k * nh + h], back)
            fwd = jnp.where(far, rb_ref[(half + k) * nh + h], fwd)
        return jnp.where(rel > 0, fwd, back) * LOG2E, r, c

    near, _, _ = tile(near_ref.shape[1:], -near_ref.shape[2])
    near_ref[0] = near
    diag, r, c = tile(diag_ref.shape[1:], 0)
    diag_ref[0] = jnp.where((c // CHUNK) <= (r // CHUNK), diag, NEG)


def _bias_tiles(rel_bias, tq, tk):
    nh = rel_bias.shape[1]
    return pl.pallas_call(
        functools.partial(_bias_kernel, nh=nh, thr=_bucket_thresholds()),
        grid=(nh,),
        in_specs=[pl.BlockSpec(memory_space=pltpu.SMEM)],
        out_specs=[pl.BlockSpec((1, tq, tk), lambda h: (h, 0, 0)),
                   pl.BlockSpec((1, tq, tq), lambda h: (h, 0, 0))],
        out_shape=[jax.ShapeDtypeStruct((nh, tq, tk), F32),
                   jax.ShapeDtypeStruct((nh, tq, tq), F32)],
        compiler_params=_cparams(1),
        name="rel_bias_tiles",
    )(rel_bias.reshape(-1))


def _diff_kernel(lam_ref, rb_ref, q_ref, kt_ref, v_ref, *rest, t, past, tq, tk, nh, out_scale):
    if past:
        kth_ref, vh_ref = rest[0], rest[1]
        rest = rest[2:]
    bn_ref, bd_ref, gm_ref, o_ref, kt_scr, v_scr = rest
    h = pl.program_id(1)
    qi = pl.program_id(2)
    nq = t // tq

    @pl.when(qi == 0)
    def _():
        if past:
            kt_scr[:, 0:past] = kth_ref[0, 0].astype(BF16)
            v_scr[0:past, :] = vh_ref[0, 0, pl.ds(h, past, stride=nh), :].astype(BF16)
        kt_scr[:, past:past + t] = kt_ref[0, 0].astype(BF16)
        v_scr[past:past + t, :] = v_ref[0, 0, pl.ds(h, t, stride=nh), :].astype(BF16)

    q = q_ref[0]
    lane = lax.broadcasted_iota(jnp.int32, q.shape, 1)
    q1 = jnp.where(lane < DQK_DIFF, q, jnp.zeros_like(q))
    q2 = jnp.where(lane >= DQK_DIFF, q, jnp.zeros_like(q))
    far_bias = rb_ref[(N_BUCKETS // 2 - 1) * nh + h] * LOG2E

    def block(carry, off, width, bias_ref):
        kt = kt_scr[:, pl.ds(off, width)]
        vb = v_scr[pl.ds(off, width), :]
        c1, c2 = carry
        pv = lambda p: _dot(p, vb)
        s1, s2 = _dot(q1, kt), _dot(q2, kt)
        if bias_ref is None:
            shift = far_bias
        else:
            shift = 0.0
            s1, s2 = s1 + bias_ref[0], s2 + bias_ref[0]
        return _online_update(*c1, s1, shift, pv), _online_update(*c2, s2, shift, pv)

    carry = (_softmax_init(tq, DV_DIFF), _softmax_init(tq, DV_DIFF))
    if nq == 1:
        if past:
            carry = block(carry, 0, past, bn_ref)
        d0 = past
    else:
        carry = lax.fori_loop(
            0, qi - 1, lambda j, cr: block(cr, pl.multiple_of(j * tk, tk), tk, None), carry)
        carry = lax.cond(qi >= 1,
                         lambda cr: block(cr, pl.multiple_of((qi - 1) * tk, tk), tk, bn_ref),
                         lambda cr: cr, carry)
        d0 = pl.multiple_of(qi * tq, tq)
    carry = block(carry, d0, tq, bd_ref)

    (_, l1, a1), (_, l2, a2) = carry
    o = a1 / l1 - lam_ref[0] * (a2 / l2)
    o_ref[0] = (_rms_rows(o) * out_scale * gm_ref[...]).astype(BF16)


def _diff_attn(q, kt, v, slot, kth, vh, li, lam, rel_bias, bias_tiles, gm, lam_init):
    bsz, t, hw = q.shape
    nh = hw // LANES
    past = 0 if kth is None else kth.shape[-1]
    tq, tk, nq = _attn_blocks(t, past)
    bias_near, bias_diag = bias_tiles
    smem = pl.BlockSpec(memory_space=pltpu.SMEM)
    args = [lam.reshape(1), rel_bias.reshape(-1), q, kt, v]
    in_specs = [smem, smem,
                pl.BlockSpec((1, tq, LANES), lambda b, h, i: (b, i, h)),
                pl.BlockSpec((1, 1, LANES, t), lambda b, h, i: (slot, b, h, 0)),
                pl.BlockSpec((1, 1, t * nh, LANES), lambda b, h, i: (slot, b, 0, 0))]
    if past:
        args += [kth, vh]
        in_specs += [pl.BlockSpec((1, 1, LANES, past), lambda b, h, i: (li, b, h, 0)),
                     pl.BlockSpec((1, 1, past * nh, LANES), lambda b, h, i: (li, b, 0, 0))]
    args += [bias_near, bias_diag, gm]
    in_specs += [pl.BlockSpec((1, tq, tk), lambda b, h, i: (h, 0, 0)),
                 pl.BlockSpec((1, tq, tq), lambda b, h, i: (h, 0, 0)),
                 pl.BlockSpec((1, LANES), lambda b, h, i: (0, h))]
    return pl.pallas_call(
        functools.partial(_diff_kernel, t=t, past=past, tq=tq, tk=tk, nh=nh, out_scale=1.0 - lam_init),
        grid=(bsz, nh, nq),
        in_specs=in_specs,
        out_specs=pl.BlockSpec((1, tq, LANES), lambda b, h, i: (b, i, h)),
        out_shape=jax.ShapeDtypeStruct((bsz, t, hw), BF16),
        scratch_shapes=[pltpu.VMEM((LANES, past + t), BF16),
                        pltpu.VMEM((past + t, LANES), BF16)],
        compiler_params=_cparams(3),
        name="diff_attn",
    )(*args)


def _logf_kernel(fg_ref, bf_ref, *rest, t, past, nh):
    if past:
        past_ref, logf_ref, cum_ref = rest
    else:
        logf_ref, cum_ref = rest
    x = fg_ref[0][:, :nh] + bf_ref[...]
    lf = -(jnp.maximum(-x, 0.0) + jnp.log1p(jnp.exp(-jnp.abs(x))))
    logf_ref[0] = lf

    def tri(n):
        r = lax.broadcasted_iota(jnp.int32, (n, n), 0)
        c = lax.broadcasted_iota(jnp.int32, (n, n), 1)
        return (r >= c).astype(F32)

    carry = jnp.zeros((1, nh), F32)
    pos = 0
    for src, length in ((None, past), (lf, t)):
        for r0 in range(0, length, LANES):
            n = min(LANES, length - r0)
            blk = past_ref[0, r0:r0 + n, :] if src is None else src[r0:r0 + n, :]
            cs = jnp.dot(tri(n), blk, preferred_element_type=F32,
                         precision=lax.Precision.HIGHEST) + carry
            cum_ref[0, pos:pos + n, :] = cs
            carry = cs[n - 1:n, :]
            pos += n


def _logf_cum(fg, b_f, past_logf):
    bsz, t, _ = fg.shape
    nh = b_f.shape[-1]
    past = 0 if past_logf is None else past_logf.shape[1]
    args = [fg, b_f.reshape(1, nh)]
    in_specs = [pl.BlockSpec((1, t, LANES), lambda b: (b, 0, 0)),
                pl.BlockSpec((1, nh), lambda b: (0, 0))]
    if past:
        args.append(past_logf)
        in_specs.append(pl.BlockSpec((1, past, nh), lambda b: (b, 0, 0)))
    return pl.pallas_call(
        functools.partial(_logf_kernel, t=t, past=past, nh=nh),
        grid=(bsz,),
        in_specs=in_specs,
        out_specs=[pl.BlockSpec((1, t, nh), lambda b: (b, 0, 0)),
                   pl.BlockSpec((1, past + t, nh), lambda b: (b, 0, 0))],
        out_shape=[jax.ShapeDtypeStruct((bsz, t, nh), F32),
                   jax.ShapeDtypeStruct((bsz, past + t, nh), F32)],
        compiler_params=_cparams(1),
        name="logf_cum",
    )(*args)


def _fox_kernel(q_ref, kt_ref, vt_ref, *rest, t, past, tq, tk):
    if past:
        kth_ref, vth_ref = rest[0], rest[1]
        rest = rest[2:]
    cq_ref, ck_ref, gm_ref, o_ref, kt_scr, vt_scr = rest
    hp = pl.program_id(1)
    qi = pl.program_id(2)
    nq = t // tq

    @pl.when(qi == 0)
    def _():
        if past:
            kt_scr[:, 0:past] = kth_ref[0, 0].astype(BF16)
            vt_scr[:, 0:past] = vth_ref[0, 0].astype(BF16)
        kt_scr[:, past:past + t] = kt_ref[0, 0].astype(BF16)
        vt_scr[:, past:past + t] = vt_ref[0, 0].astype(BF16)

    q = q_ref[0]
    lane = lax.broadcasted_iota(jnp.int32, q.shape, 1)
    cq_blk = cq_ref[0]
    head_lane = lax.broadcasted_iota(jnp.int32, cq_blk.shape, 1)
    r = lax.broadcasted_iota(jnp.int32, (tq, tq), 0)
    c = lax.broadcasted_iota(jnp.int32, (tq, tq), 1)
    causal = c <= r

    def head_row(blk, head):
        rid = lax.broadcasted_iota(jnp.int32, blk.shape, 0)
        return jnp.sum(jnp.where(rid == head, blk, 0.0), axis=0, keepdims=True)

    qh, cq = [], []
    for hh in range(2):
        in_half = (lane >= hh * HD_FOX) & (lane < (hh + 1) * HD_FOX)
        qh.append(jnp.where(in_half, q, jnp.zeros_like(q)))
        cq.append(jnp.sum(jnp.where(head_lane == 2 * hp + hh, cq_blk, 0.0), axis=-1, keepdims=True) * LOG2E)

    def block(carry, off, width, mask):
        kt = kt_scr[:, pl.ds(off, width)]
        vt = vt_scr[:, pl.ds(off, width)]
        ck_all = ck_ref[0, :, pl.ds(off, width)] * LOG2E
        pv = lambda p: _dot_nt(p, vt)
        out = []
        for hh in range(2):
            s = _dot(qh[hh], kt) - head_row(ck_all, 2 * hp + hh)
            if mask is not None:
                s = jnp.where(mask, s, NEG)
            out.append(_online_update(*carry[hh], s, cq[hh], pv))
        return tuple(out)

    carry = (_softmax_init(tq, LANES), _softmax_init(tq, LANES))
    if nq == 1:
        if past:
            carry = block(carry, 0, past, None)
        d0 = past
    else:
        carry = lax.fori_loop(
            0, qi, lambda j, cr: block(cr, pl.multiple_of(j * tk, tk), tk, None), carry)
        d0 = pl.multiple_of(qi * tq, tq)
    carry = block(carry, d0, tq, causal)

    (_, l0, a0), (_, l1, a1) = carry
    o = jnp.where(lane < HD_FOX, a0 / l0, a1 / l1)
    sq = o * o
    ms0 = jnp.sum(jnp.where(lane < HD_FOX, sq, 0.0), axis=-1, keepdims=True) / HD_FOX
    ms1 = jnp.sum(jnp.where(lane >= HD_FOX, sq, 0.0), axis=-1, keepdims=True) / HD_FOX
    inv = jnp.where(lane < HD_FOX, lax.rsqrt(ms0 + EPS), lax.rsqrt(ms1 + EPS))
    o_ref[0] = (o * inv * gm_ref[...]).astype(BF16)


def _fox_attn(q, kt, vt, slot, kth, vth, li, cum, gm):
    bsz, t, hw = q.shape
    npair = hw // LANES
    past = 0 if kth is None else kth.shape[-1]
    nh = cum.shape[-1]
    tq, tk, nq = _attn_blocks(t, past)
    cq = cum[:, past:, :]
    ck = jnp.swapaxes(cum, 1, 2)
    kv_spec = pl.BlockSpec((1, 1, LANES, t), lambda b, h, i: (slot, b, h, 0))
    args = [q, kt, vt]
    in_specs = [pl.BlockSpec((1, tq, LANES), lambda b, h, i: (b, i, h)), kv_spec, kv_spec]
    if past:
        h_spec = pl.BlockSpec((1, 1, LANES, past), lambda b, h, i: (li, b, h, 0))
        args += [kth, vth]
        in_specs += [h_spec, h_spec]
    args += [cq, ck, gm]
    in_specs += [pl.BlockSpec((1, tq, nh), lambda b, h, i: (b, i, 0)),
                 pl.BlockSpec((1, nh, past + t), lambda b, h, i: (b, 0, 0)),
                 pl.BlockSpec((1, LANES), lambda b, h, i: (0, h))]
    return pl.pallas_call(
        functools.partial(_fox_kernel, t=t, past=past, tq=tq, tk=tk),
        grid=(bsz, npair, nq),
        in_specs=in_specs,
        out_specs=pl.BlockSpec((1, tq, LANES), lambda b, h, i: (b, i, h)),
        out_shape=jax.ShapeDtypeStruct((bsz, t, hw), BF16),
        scratch_shapes=[pltpu.VMEM((LANES, past + t), BF16),
                        pltpu.VMEM((LANES, past + t), BF16)],
        compiler_params=_cparams(3),
        name="fox_attn",
    )(*args)


def _oproj_kernel(x_ref, a_ref, b_ref, c_ref, w_ref, o_ref, *, d1, d2):
    acc = _dot(a_ref[...], w_ref[0:d1, :])
    acc += _dot(b_ref[...], w_ref[d1:d2, :])
    acc += _dot(c_ref[...], w_ref[d2:, :])
    o_ref[...] = x_ref[...] + acc


def _oproj(x, a, b, c, w):
    n, d = x.shape
    tm = min(ROW_BLOCK, n)
    d1 = a.shape[1]
    d2 = d1 + b.shape[1]
    row = lambda width: pl.BlockSpec((tm, width), lambda i: (i, 0))
    return pl.pallas_call(
        functools.partial(_oproj_kernel, d1=d1, d2=d2),
        grid=(n // tm,),
        in_specs=[row(d), row(a.shape[1]), row(b.shape[1]), row(c.shape[1]), _resident(w.shape)],
        out_specs=row(d),
        out_shape=jax.ShapeDtypeStruct((n, d), F32),
        compiler_params=_cparams(1),
        name="oproj",
    )(x, a, b, c, w)


def _ffn_kernel(x_ref, g_ref, wa_ref, wb_ref, cw_ref, cb_ref, wo_ref, st_ref, o_ref, tail_ref,
                xn_scr, halo_scr, *, tm, t, tf):
    i = pl.program_id(0)
    c = pl.program_id(1)

    @pl.when(c == 0)
    def _():
        x = x_ref[...]
        xn_scr[...] = (_rms_rows(x) * g_ref[...]).astype(BF16)
        o_ref[...] = x

    tail_rows = 8
    if t >= tm:
        @pl.when((i % (t // tm)) == 0)
        def _():
            halo_scr[c] = st_ref[0]

        halo = halo_scr[c]
    else:
        st = st_ref[...]
        nseq = tm // t

    xn = xn_scr[...]
    rows8 = lax.broadcasted_iota(jnp.int32, (tail_rows, FF_SUB), 0)
    rows = lax.broadcasted_iota(jnp.int32, (tm, FF_SUB), 0)
    gated = []
    for k in range(tf // FF_SUB):
        cols = slice(k * FF_SUB, (k + 1) * FF_SUB)
        a = _dot(xn, wa_ref[:, cols])
        b = _dot(xn, wb_ref[:, cols])
        prev1 = pltpu.roll(a, 1, 0)
        prev2 = pltpu.roll(a, 2, 0)
        if t >= tm:
            h0 = halo[tail_rows - 2:tail_rows - 1, cols]
            h1 = halo[tail_rows - 1:tail_rows, cols]
            head1 = jnp.where(rows8 == 0, h1, prev1[:tail_rows])
            head2 = jnp.where(rows8 == 0, h0, jnp.where(rows8 == 1, h1, prev2[:tail_rows]))
            a1 = jnp.concatenate([head1, prev1[tail_rows:]], axis=0)
            a2 = jnp.concatenate([head2, prev2[tail_rows:]], axis=0)
            last = a[tm - tail_rows:tm, :]
            halo_scr[c, :, cols] = last
            tail_ref[0, :, cols] = last
        else:
            h0 = jnp.broadcast_to(st[:, tail_rows - 2:tail_rows - 1, cols], (nseq, t, FF_SUB)).reshape(tm, FF_SUB)
            h1 = jnp.broadcast_to(st[:, tail_rows - 1:tail_rows, cols], (nseq, t, FF_SUB)).reshape(tm, FF_SUB)
            rmod = rows & (t - 1)
            a1 = jnp.where(rmod == 0, h1, prev1)
            a2 = jnp.where(rmod == 0, h0, jnp.where(rmod == 1, h1, prev2))
            tail_ref[:, :, cols] = a.reshape(nseq, t, FF_SUB)[:, t - tail_rows:, :]
        ac = cw_ref[2:3, cols] * a + cw_ref[1:2, cols] * a1 + cw_ref[0:1, cols] * a2 + cb_ref[:, cols]
        gated.append((ac * (1.0 / (1.0 + jnp.exp(-ac))) * b).astype(BF16))
    o_ref[...] += _dot(jnp.concatenate(gated, axis=1), wo_ref[...])


def _ffn(x, t, g, wa, wb, cw, cb, wo, st):
    n, d = x.shape
    ffp = wo.shape[0]
    tf = min(FF_BLOCK, ffp)
    tm = min(FF_ROWS, n)
    assert ffp % tf == 0 and tf % FF_SUB == 0 and n % tm == 0 and (t % tm == 0 or tm % t == 0)
    assert t & (t - 1) == 0
    nseq = n // t
    if t >= tm:
        per = t // tm
        st_spec = pl.BlockSpec((1, 8, tf), lambda i, c: (i // per, 0, c))
        tail_spec = pl.BlockSpec((1, 8, tf), lambda i, c: (i, 0, c))
    else:
        per = 1
        st_spec = tail_spec = pl.BlockSpec((tm // t, 8, tf), lambda i, c: (i, 0, c))
    x_new, tail = pl.pallas_call(
        functools.partial(_ffn_kernel, tm=tm, t=t, tf=tf),
        grid=(n // tm, ffp // tf),
        in_specs=[pl.BlockSpec((tm, d), lambda i, c: (i, 0)),
                  pl.BlockSpec((1, d), lambda i, c: (0, 0)),
                  pl.BlockSpec((d, tf), lambda i, c: (0, c)),
                  pl.BlockSpec((d, tf), lambda i, c: (0, c)),
                  pl.BlockSpec((8, tf), lambda i, c: (0, c)),
                  pl.BlockSpec((1, tf), lambda i, c: (0, c)),
                  pl.BlockSpec((tf, d), lambda i, c: (c, 0)),
                  st_spec],
        out_specs=[pl.BlockSpec((tm, d), lambda i, c: (i, 0)), tail_spec],
        out_shape=[jax.ShapeDtypeStruct((n, d), F32),
                   jax.ShapeDtypeStruct((nseq * per, 8, ffp), F32)],
        scratch_shapes=[pltpu.VMEM((tm, d), BF16),
                        pltpu.VMEM((ffp // tf, 8, tf), F32)],
        compiler_params=_cparams(2),
        name="ffn",
    )(x, g, wa, wb, cw, cb, wo, st)
    return x_new, tail.reshape(nseq, per, 8, ffp)[:, per - 1]


def _norm_kernel(x_ref, g_ref, o_ref):
    o_ref[...] = _rms_rows(x_ref[...]) * g_ref[...]


def _final_norm(x, g):
    n, d = x.shape
    tm = min(ROW_BLOCK, n)
    return pl.pallas_call(
        _norm_kernel,
        grid=(n // tm,),
        in_specs=[pl.BlockSpec((tm, d), lambda i: (i, 0)), pl.BlockSpec((1, d), lambda i: (0, 0))],
        out_specs=pl.BlockSpec((tm, d), lambda i: (i, 0)),
        out_shape=jax.ShapeDtypeStruct((n, d), F32),
        compiler_params=_cparams(1),
        name="final_norm",
    )(x, g)


def _prep_layer(p, dims):
    d_ssm, hd, hf, d_ff, ffp = dims
    wd, wf = hd * LANES, hf * HD_FOX
    o_qd, o_kd, o_vd = d_ssm, d_ssm + wd, d_ssm + 2 * wd
    o_qf, o_kf, o_vf, o_fg = d_ssm + 3 * wd, d_ssm + 3 * wd + wf, d_ssm + 3 * wd + 2 * wf, d_ssm + 3 * wd + 3 * wf
    w_in = p['w_in']
    w_fg = jnp.pad(w_in[:, o_fg:], ((0, 0), (0, LANES - hf)))
    q = dict(p)
    q['w_in_n'] = jnp.concatenate([w_in[:, o_qd:o_kd], w_in[:, o_qf:o_kf], w_fg, w_in[:, o_vd:o_qf]],
                                  axis=1).astype(BF16)
    q['w_in_t'] = jnp.concatenate([w_in[:, :o_qd], w_in[:, o_kd:o_vd], w_in[:, o_kf:o_fg]],
                                  axis=1).T.astype(BF16)
    q['w_glu'] = p['w_glu'].astype(BF16)
    q['w_o'] = p['w_o'].astype(BF16)
    padc = ((0, 0), (0, ffp - d_ff))
    q['w_a'] = jnp.pad(p['w_ffn_in'][:, :d_ff], padc).astype(BF16)
    q['w_b'] = jnp.pad(p['w_ffn_in'][:, d_ff:], padc).astype(BF16)
    q['w_ffn_out'] = jnp.pad(p['w_ffn_out'], ((0, ffp - d_ff), (0, 0))).astype(BF16)
    q['conv_w'] = jnp.pad(p['ffn_conv_w'], ((0, 8 - CONV_W), (0, ffp - d_ff)))
    q['conv_b'] = jnp.pad(p['ffn_conv_b'], (0, ffp - d_ff))[None, :]
    s5 = _s5_params(p['ssm_lam_re'], p['ssm_lam_im'], p['ssm_log_dt'], p['ssm_b_re'], p['ssm_b_im'],
                    p['ssm_c_re'], p['ssm_c_im'], p['ssm_d'], (S5_BLOCK, CHUNK))
    q['s5_seq'], q['s5_short'] = s5[S5_BLOCK], s5[CHUNK]
    q['lam'] = (jnp.exp(jnp.sum(p['diff_lam_q1'] * p['diff_lam_k1']))
                - jnp.exp(jnp.sum(p['diff_lam_q2'] * p['diff_lam_k2'])) + p['lam_init'])
    return q


def _layer(x, bsz, t, li, hist, st, p, dims, bias_tiles, lam_init, slot, nslots, stacks):
    d_ssm, hd, hf, d_ff, ffp = dims
    kdt_h, vd_h, kft_h, vft_h = hist
    lf_c, sr_c, si_c, cv_c = st
    n, d = x.shape
    wd, wf = hd * LANES, hf * HD_FOX
    u, qd, qf, fg, kv = _inproj(x, t, p['g_norm_mix'][None, :], p['w_in_n'], p['w_in_t'],
                                d_ssm, wd, wf, slot, nslots, stacks)
    vd, kdt, kft, vft = kv
    gm = p['g_mix_out'][None, :]
    if t >= min(ROW_BLOCK, n):
        y_ssm, sr, si = _s5_seq(u, sr_c, si_c, p['s5_seq'])
    else:
        y_ssm, sr, si = _s5_short(u[0], sr_c, si_c, p['s5_short'])
    o_ssm = _glu(y_ssm, p['w_glu'], p['b_glu'][None, :], gm[:, :d_ssm])

    o_diff = _diff_attn(qd.reshape(bsz, t, wd), kdt, vd.reshape(nslots, bsz, t * hd, DV_DIFF), slot,
                        kdt_h, vd_h, li, p['lam'], p['rel_bias'], bias_tiles,
                        gm[:, d_ssm:d_ssm + wd], lam_init)

    logf, cum = _logf_cum(fg.reshape(bsz, t, LANES), p['fox_b_f'], lf_c)
    o_fox = _fox_attn(qf.reshape(bsz, t, wf), kft, vft, slot, kft_h, vft_h, li, cum, gm[:, d_ssm + wd:])

    x = _oproj(x, o_ssm, o_diff.reshape(n, wd), o_fox.reshape(n, wf), p['w_o'])

    x, tail = _ffn(x, t, p['g_norm_ffn'][None, :], p['w_a'], p['w_b'], p['conv_w'], p['conv_b'],
                   p['w_ffn_out'], cv_c)
    return x, kv, (logf, sr, si, tail[:, 8 - (CONV_W - 1):, :d_ff])


def kernel(x_prompt, x_sample, cache_diff_k, cache_diff_v, cache_fox_k, cache_fox_v, cache_fox_logf, state_ssm_re, state_ssm_im, state_ffn_conv, g_norm_mix, w_in, ssm_lam_re, ssm_lam_im, ssm_log_dt, ssm_b_re, ssm_b_im, ssm_c_re, ssm_c_im, ssm_d, w_glu, b_glu, diff_lam_q1, diff_lam_k1, diff_lam_q2, diff_lam_k2, rel_bias, fox_b_f, g_mix_out, w_o, g_norm_ffn, w_ffn_in, ffn_conv_w, ffn_conv_b, w_ffn_out, g_final):
    per_layer = {'g_norm_mix': g_norm_mix, 'w_in': w_in, 'ssm_lam_re': ssm_lam_re,
                 'ssm_lam_im': ssm_lam_im, 'ssm_log_dt': ssm_log_dt, 'ssm_b_re': ssm_b_re,
                 'ssm_b_im': ssm_b_im, 'ssm_c_re': ssm_c_re, 'ssm_c_im': ssm_c_im, 'ssm_d': ssm_d,
                 'w_glu': w_glu, 'b_glu': b_glu, 'diff_lam_q1': diff_lam_q1, 'diff_lam_k1': diff_lam_k1,
                 'diff_lam_q2': diff_lam_q2, 'diff_lam_k2': diff_lam_k2, 'fox_b_f': fox_b_f,
                 'g_mix_out': g_mix_out, 'w_o': w_o, 'g_norm_ffn': g_norm_ffn, 'w_ffn_in': w_ffn_in,
                 'ffn_conv_w': ffn_conv_w, 'ffn_conv_b': ffn_conv_b, 'w_ffn_out': w_ffn_out}
    depth = w_in.shape[0]
    bp, tp, d_model = x_prompt.shape
    bs, ts, _ = x_sample.shape
    past = cache_diff_k.shape[2]
    g_ssm = ssm_lam_re.shape[1]
    d_ssm = g_ssm * SSM_GROUP
    hd = cache_diff_k.shape[3]
    hf = cache_fox_k.shape[3]
    wd, wf = hd * LANES, hf * HD_FOX
    d_ff = w_ffn_out.shape[1]
    ffp = -(-d_ff // FF_BLOCK) * FF_BLOCK if d_ff > FF_BLOCK else d_ff
    dims = (d_ssm, hd, hf, d_ff, ffp)

    def conv_state_block(s):
        return jnp.pad(s, ((0, 0), (8 - (CONV_W - 1), 0), (0, ffp - d_ff)))

    hist_sample = (jnp.transpose(cache_diff_k, (0, 1, 3, 4, 5, 2)).reshape(depth, bs, wd, past),
                   cache_diff_v.reshape(depth, bs, past * hd, DV_DIFF),
                   jnp.transpose(cache_fox_k, (0, 1, 3, 4, 2)).reshape(depth, bs, wf, past),
                   jnp.transpose(cache_fox_v, (0, 1, 3, 4, 2)).reshape(depth, bs, wf, past))
    hist_prompt = (None, None, None, None)
    st_prompt = (None, jnp.zeros((bp, g_ssm, P_SSM), F32), jnp.zeros((bp, g_ssm, P_SSM), F32),
                 conv_state_block(jnp.zeros((bp, CONV_W - 1, d_ff), F32)))
    bias_p = _bias_tiles(rel_bias, *_attn_blocks(tp, 0)[:2])
    bias_s = _bias_tiles(rel_bias, *_attn_blocks(ts, past)[:2])

    xp = x_prompt.reshape(bp * tp, d_model)
    xs = x_sample.reshape(bs * ts, d_model)
    kv_p = (jnp.zeros((depth, bp * tp * hd, DV_DIFF), F32), jnp.zeros((depth, bp, wd, tp), F32),
            jnp.zeros((depth, bp, wf, tp), F32), jnp.zeros((depth, bp, wf, tp), F32))
    kv_s, outs_p, outs_s = [], [], []
    for li in range(depth):
        lam_init = 0.8 - 0.6 * math.exp(-0.3 * li)
        p = {name: arr[li] for name, arr in per_layer.items()}
        p['rel_bias'] = rel_bias
        p['lam_init'] = lam_init
        p = _prep_layer(p, dims)
        xp, kv_p, sp = _layer(xp, bp, tp, li, hist_prompt, st_prompt, p, dims, bias_p, lam_init,
                              li, depth, kv_p)
        st_sample = (cache_fox_logf[li], state_ssm_re[li], state_ssm_im[li],
                     conv_state_block(state_ffn_conv[li]))
        xs, kv, ss = _layer(xs, bs, ts, li, hist_sample, st_sample, p, dims, bias_s, lam_init,
                            0, 1, None)
        kv_s.append(kv)
        outs_p.append(sp)
        outs_s.append(ss)
    kv_s = [jnp.concatenate([kv[i] for kv in kv_s]) for i in range(4)]

    def assemble(kv, outs, bsz, t):
        vd, kdt, kft, vft = kv
        logf, sr, si, cv = [jnp.stack([s[i] for s in outs]) for i in range(4)]
        return (jnp.transpose(kdt.reshape(depth, bsz, hd, 2, DQK_DIFF, t), (0, 1, 5, 2, 3, 4)),
                vd.reshape(depth, bsz, t, hd, DV_DIFF),
                jnp.transpose(kft.reshape(depth, bsz, hf, HD_FOX, t), (0, 1, 4, 2, 3)),
                jnp.transpose(vft.reshape(depth, bsz, hf, HD_FOX, t), (0, 1, 4, 2, 3)),
                logf, sr, si, cv)

    y_prompt = _final_norm(xp, g_final[None, :]).reshape(bp, tp, d_model)
    y_sample = _final_norm(xs, g_final[None, :]).reshape(bs, ts, d_model)
    return (y_prompt, y_sample, *assemble(kv_p, outs_p, bp, tp), *assemble(kv_s, outs_s, bs, ts))
```
